```python
import math
import jax, jax.numpy as jnp
from jax import lax
import numpy as np

D_MODEL = 1024
BATCH = 16
SEQ = 4096
DEPTH = 4

D_MIX = D_MODEL
HG_WIDTH = D_MIX // 2
HG_HEAD_DIM = 128
HG_HEADS = HG_WIDTH // HG_HEAD_DIM
HG_CHUNK = 64
NSA_WIDTH = D_MIX - HG_WIDTH
NSA_HEAD_DIM = 64
NSA_HEADS = NSA_WIDTH // NSA_HEAD_DIM
NSA_KV_GROUPS = 2
NSA_HPG = NSA_HEADS // NSA_KV_GROUPS
CMP_BLOCK = 32
CMP_STRIDE = 16
CMP_HIDDEN = 128
SLC_BLOCK = 64
SLC_TOPK = 16
SLC_LOCAL = 2
WINDOW = 512
Q_BLOCK = 64
N_BUCKETS = 32
MAX_DISTANCE = 128
D_FF = ((8 * D_MODEL // 3 + 255) // 256) * 256
ALPHA = (2.0 * DEPTH) ** 0.25
BETA = (8.0 * DEPTH) ** -0.25
KV_W = NSA_KV_GROUPS * NSA_HEAD_DIM
SPLIT_SIZES = (HG_WIDTH, HG_WIDTH, HG_WIDTH, HG_WIDTH, NSA_WIDTH, KV_W, KV_W, KV_W, KV_W, KV_W, KV_W, 3 * NSA_HEADS)
D_IN = sum(SPLIT_SIZES)
NEG = -1e30
BIG = 1e9

kernel_name = "hgrn2_nsa_parallel_heads_deepnorm"


def _split_points():
    pts, acc = [], 0
    for s in SPLIT_SIZES[:-1]:
        acc += s
        pts.append(acc)
    return pts


def layer_norm(x, g, b, eps=1e-5):
    xf = x.astype(jnp.float32)
    mu = jnp.mean(xf, -1, keepdims=True)
    var = jnp.mean(jnp.square(xf - mu), -1, keepdims=True)
    return ((xf - mu) * lax.rsqrt(var + eps) * g + b).astype(x.dtype)


def hgrn2_lower_bounds(lb_param):
    p = jax.nn.softmax(lb_param.astype(jnp.float32), axis=0)
    c = jnp.cumsum(p, axis=0)
    return c - c[0:1]


def hgrn2_mixer(q, f_logit, i, g, lb, gnorm_w):
    B, S, _ = q.shape
    H, Dk, C = HG_HEADS, HG_HEAD_DIM, HG_CHUNK
    N = S // C
    f32 = jnp.float32
    logf = jnp.logaddexp(jnp.log(lb), jnp.log1p(-lb) + jax.nn.log_sigmoid(f_logit.astype(f32)))
    k = -jnp.expm1(logf)

    def to_chunks(t):
        return t.astype(f32).reshape(B, N, C, H, -1).transpose(1, 0, 3, 2, 4)

    qc, kc, vc = to_chunks(q), to_chunks(k), to_chunks(i)
    bc = jnp.cumsum(to_chunks(logf), axis=3)
    causal = jnp.tril(jnp.ones((C, C), bool))

    def step(state, xs):
        qb, kb, vb, bb = xs
        rel = bb[:, :, :, None, :] - bb[:, :, None, :, :]
        decay = jnp.exp(jnp.where(causal[:, :, None], rel, -jnp.inf))
        scores = jnp.einsum('bhtd,bhsd,bhtsd->bhts', qb, kb, decay)
        o_intra = jnp.einsum('bhts,bhsv->bhtv', scores, vb)
        o_inter = jnp.einsum('bhtd,bhdv->bhtv', qb * jnp.exp(bb), state)
        b_last = bb[:, :, -1:, :]
        k_dec = kb * jnp.exp(b_last - bb)
        new_state = state * jnp.exp(b_last[:, :, 0, :, None]) + jnp.einsum('bhsd,bhsv->bhdv', k_dec, vb)
        return new_state, o_intra + o_inter

    s0 = jnp.zeros((B, H, Dk, Dk), f32)
    _, o = lax.scan(step, s0, (qc, kc, vc, bc))
    o = o.transpose(1, 0, 3, 2, 4).reshape(B, S, H, Dk)
    o = o * lax.rsqrt(jnp.mean(o * o, -1, keepdims=True) + 1e-6) * gnorm_w
    o = o.reshape(B, S, HG_WIDTH) * jax.nn.silu(g.astype(f32))
    return o.astype(q.dtype)


def t5_bucket(n):
    n = jnp.maximum(n, 0)
    max_exact = N_BUCKETS // 2
    nf = jnp.maximum(n, 1).astype(jnp.float32)
    large = max_exact + (jnp.log(nf / max_exact) / math.log(MAX_DISTANCE / max_exact)
                         * (N_BUCKETS - max_exact)).astype(jnp.int32)
    large = jnp.minimum(large, N_BUCKETS - 1)
    return jnp.where(n < max_exact, n, large)


def masked_softmax(s, valid):
    s = jnp.where(valid, s, NEG)
    p = jax.nn.softmax(s, axis=-1)
    return jnp.where(valid, p, 0.0)


def nsa_mixer(q, k_cmp, v_cmp, k_slc, v_slc, k_win, v_win, gates,
              cmp_pos_k, cmp_w1_k, cmp_w2_k, cmp_pos_v, cmp_w1_v, cmp_w2_v, rel_bias):
    B, S, _ = q.shape
    G, HPG, Dh = NSA_KV_GROUPS, NSA_HPG, NSA_HEAD_DIM
    f32 = jnp.float32
    scale = Dh ** -0.5
    rb = rel_bias.astype(f32)
    qh = q.astype(f32).reshape(B, S, G, HPG, Dh).transpose(0, 2, 3, 1, 4)
    gh = jax.nn.sigmoid(gates.astype(f32)).reshape(B, S, 3, G, HPG).transpose(0, 3, 4, 1, 2)

    def kv(t):
        return t.astype(f32).reshape(B, S, G, Dh).transpose(0, 2, 1, 3)

    n_cmp = (S - CMP_BLOCK) // CMP_STRIDE + 1

    def compress(t, pos, w1, w2):
        c = kv(t).reshape(B, G, S // CMP_STRIDE, CMP_STRIDE, Dh)
        blocks = jnp.concatenate([c[:, :, :-1], c[:, :, 1:]], axis=3)
        blocks = (blocks + pos).reshape(B, G, n_cmp, CMP_BLOCK * Dh)
        return jax.nn.gelu(blocks @ w1) @ w2

    kc = compress(k_cmp, cmp_pos_k, cmp_w1_k, cmp_w2_k)
    vc = compress(v_cmp, cmp_pos_v, cmp_w1_v, cmp_w2_v)
    cmp_start = jnp.arange(n_cmp) * CMP_STRIDE
    cmp_end = cmp_start + CMP_BLOCK - 1

    n_slc = S // SLC_BLOCK
    n_sel = min(SLC_TOPK, n_slc)
    ks_blocks = kv(k_slc).reshape(B, G, n_slc, SLC_BLOCK, Dh)
    vs_blocks = kv(v_slc).reshape(B, G, n_slc, SLC_BLOCK, Dh)
    slc_start = jnp.arange(n_slc) * SLC_BLOCK
    overlap = jnp.clip(jnp.minimum(cmp_start[:, None] + CMP_BLOCK, slc_start[None, :] + SLC_BLOCK)
                       - jnp.maximum(cmp_start[:, None], slc_start[None, :]), 0, None)
    agg = overlap.astype(f32) / CMP_STRIDE

    kw_pad = jnp.pad(kv(k_win), ((0, 0), (0, 0), (WINDOW, 0), (0, 0)))
    vw_pad = jnp.pad(kv(v_win), ((0, 0), (0, 0), (WINDOW, 0), (0, 0)))
    tab_g = rb.reshape(N_BUCKETS, G, HPG)
    bi = jnp.arange(B)[:, None, None, None]
    gi = jnp.arange(G)[None, :, None, None]

    def head_bias(dist):
        bsh = rb[t5_bucket(dist)]
        return bsh.reshape(dist.shape + (G, HPG)).transpose(2, 3, 0, 1)

    def block_fn(qb_idx):
        t0 = qb_idx * Q_BLOCK
        t_pos = t0 + jnp.arange(Q_BLOCK)
        qblk = lax.dynamic_slice_in_dim(qh, t0, Q_BLOCK, axis=3)
        gblk = lax.dynamic_slice_in_dim(gh, t0, Q_BLOCK, axis=3)
        s_c = jnp.einsum('bghtd,bgnd->bghtn', qblk, kc) * scale + head_bias(t_pos[:, None] - cmp_end[None, :])
        valid_c = cmp_end[None, :] <= t_pos[:, None]
        p_c = masked_softmax(s_c, valid_c)
        o_c = jnp.einsum('bghtn,bgnd->bghtd', p_c, vc)
        imp = jnp.einsum('bghtn,nj->bgtj', p_c, agg)
        cur = t_pos // SLC_BLOCK
        j = jnp.arange(n_slc)
        diff = cur[:, None] - j[None, :]
        forced = (j[None, :] == 0) | ((diff >= 0) & (diff < SLC_LOCAL))
        causal_blk = slc_start[None, :] <= t_pos[:, None]
        score = jnp.where(forced, BIG, jnp.where(causal_blk, imp, -BIG))
        _, idx = lax.top_k(score, n_sel)
        L = n_sel * SLC_BLOCK
        ks_g = ks_blocks[bi, gi, idx].reshape(B, G, Q_BLOCK, L, Dh)
        vs_g = vs_blocks[bi, gi, idx].reshape(B, G, Q_BLOCK, L, Dh)
        key_pos = (idx[..., None] * SLC_BLOCK + jnp.arange(SLC_BLOCK)).reshape(B, G, Q_BLOCK, L)
        dist_s = t_pos[None, None, :, None] - key_pos
        bias_s = jax.vmap(lambda tb, bk: tb[bk], in_axes=(1, 1), out_axes=1)(tab_g, t5_bucket(dist_s))
        bias_s = bias_s.transpose(0, 1, 4, 2, 3)
        s_s = jnp.einsum('bghtd,bgtld->bghtl', qblk, ks_g) * scale + bias_s
        p_s = masked_softmax(s_s, (dist_s >= 0)[:, :, None])
        o_s = jnp.einsum('bghtl,bgtld->bghtd', p_s, vs_g)
        kw = lax.dynamic_slice_in_dim(kw_pad, t0, WINDOW + Q_BLOCK, axis=2)
        vw = lax.dynamic_slice_in_dim(vw_pad, t0, WINDOW + Q_BLOCK, axis=2)
        kpos = t0 - WINDOW + jnp.arange(WINDOW + Q_BLOCK)
        dist_w = t_pos[:, None] - kpos[None, :]
        valid_w = (dist_w >= 0) & (dist_w < WINDOW) & (kpos[None, :] >= 0)
        s_w = jnp.einsum('bghtd,bgkd->bghtk', qblk, kw) * scale + head_bias(dist_w)
        p_w = masked_softmax(s_w, valid_w)
        o_w = jnp.einsum('bghtk,bgkd->bghtd', p_w, vw)
        return gblk[..., 0:1] * o_c + gblk[..., 1:2] * o_s + gblk[..., 2:3] * o_w

    o = lax.map(block_fn, jnp.arange(S // Q_BLOCK))
    o = o.transpose(1, 0, 4, 2, 3, 5).reshape(B, S, NSA_WIDTH)
    return o.astype(q.dtype)


def setup_inputs(seed: int = 0) -> dict:
    key = jax.random.key(seed)
    ks = jax.random.split(key, 20)
    f32 = jnp.float32

    def nrm(k, shape, fan_in, scale=1.0):
        return jax.random.normal(k, shape, f32) * (scale * fan_in ** -0.5)

    def small(k, shape, s):
        return jax.random.normal(k, shape, f32) * s

    return {
        "x": jax.random.normal(ks[0], (BATCH, SEQ, D_MODEL), f32),
        "w_in": nrm(ks[1], (DEPTH, D_MODEL, D_IN), D_MODEL),
        "hg_lb_param": small(ks[2], (DEPTH, HG_WIDTH), 0.1),
        "hg_norm_w": 1.0 + small(ks[3], (DEPTH, HG_HEAD_DIM), 0.02),
        "cmp_pos_k": small(ks[4], (DEPTH, CMP_BLOCK, NSA_HEAD_DIM), 0.02),
        "cmp_w1_k": nrm(ks[5], (DEPTH, CMP_BLOCK * NSA_HEAD_DIM, CMP_HIDDEN), CMP_BLOCK * NSA_HEAD_DIM),
        "cmp_w2_k": nrm(ks[6], (DEPTH, CMP_HIDDEN, NSA_HEAD_DIM), CMP_HIDDEN),
        "cmp_pos_v": small(ks[7], (DEPTH, CMP_BLOCK, NSA_HEAD_DIM), 0.02),
        "cmp_w1_v": nrm(ks[8], (DEPTH, CMP_BLOCK * NSA_HEAD_DIM, CMP_HIDDEN), CMP_BLOCK * NSA_HEAD_DIM),
        "cmp_w2_v": nrm(ks[9], (DEPTH, CMP_HIDDEN, NSA_HEAD_DIM), CMP_HIDDEN),
        "rel_bias": small(ks[10], (N_BUCKETS, NSA_HEADS), 0.1),
        "w_out": nrm(ks[11], (DEPTH, D_MIX, D_MODEL), D_MIX, BETA),
        "ln1_g": 1.0 + small(ks[12], (DEPTH, D_MODEL), 0.02),
        "ln1_b": small(ks[13], (DEPTH, D_MODEL), 0.02),
        "w_ffn_gate": nrm(ks[14], (DEPTH, D_MODEL, D_FF), D_MODEL),
        "w_ffn_up": nrm(ks[15], (DEPTH, D_MODEL, D_FF), D_MODEL),
        "w_ffn_down": nrm(ks[16], (DEPTH, D_FF, D_MODEL), D_FF, BETA),
        "ln2_g": 1.0 + small(ks[17], (DEPTH, D_MODEL), 0.02),
        "ln2_b": small(ks[18], (DEPTH, D_MODEL), 0.02),
    }


def reference(x, w_in, hg_lb_param, hg_norm_w, cmp_pos_k, cmp_w1_k, cmp_w2_k,
              cmp_pos_v, cmp_w1_v, cmp_w2_v, rel_bias, w_out, ln1_g, ln1_b,
              w_ffn_gate, w_ffn_up, w_ffn_down, ln2_g, ln2_b):
    lbs = hgrn2_lower_bounds(hg_lb_param)
    pts = _split_points()
    for l in range(DEPTH):
        proj = x @ w_in[l]
        hq, hf, hi, hg, nq, kc, vc, ks, vs, kw, vw, gt = jnp.split(proj, pts, axis=-1)
        o_a = hgrn2_mixer(hq, hf, hi, hg, lbs[l], hg_norm_w[l])
        o_b = nsa_mixer(nq, kc, vc, ks, vs, kw, vw, gt,
                        cmp_pos_k[l], cmp_w1_k[l], cmp_w2_k[l],
                        cmp_pos_v[l], cmp_w1_v[l], cmp_w2_v[l], rel_bias)
        mix = jnp.concatenate([o_a, o_b], axis=-1) @ w_out[l]
        x = layer_norm(ALPHA * x + mix, ln1_g[l], ln1_b[l])
        ffn = (jax.nn.silu(x @ w_ffn_gate[l]) * (x @ w_ffn_up[l])) @ w_ffn_down[l]
        x = layer_norm(ALPHA * x + ffn, ln2_g[l], ln2_b[l])
    return x
```

```python
import functools
import math

import numpy as np
import jax
import jax.numpy as jnp
from jax import lax
from jax.experimental import pallas as pl
from jax.experimental.pallas import tpu as pltpu

F32 = jnp.float32
MXU_DT = jnp.bfloat16

D_MODEL = 1024
DEPTH = 4
HG_WIDTH = 512
HG_HEAD_DIM = 128
HG_HEADS = 4
HG_CHUNK = 64
NSA_WIDTH = 512
NSA_HEAD_DIM = 64
NSA_HEADS = 8
NSA_KV_GROUPS = 2
NSA_HPG = 4
CMP_BLOCK = 32
CMP_STRIDE = 16
CMP_HIDDEN = 128
SLC_BLOCK = 64
SLC_TOPK = 16
SLC_LOCAL = 2
WINDOW = 512
N_BUCKETS = 32
MAX_DISTANCE = 128
D_FF = 2816
ALPHA = (2.0 * DEPTH) ** 0.25
KV_W = NSA_KV_GROUPS * NSA_HEAD_DIM
LANES = 128
BIG = 1e9
MASK_NEG = -1e30
SEL_NEG = -30000.0

VMEM_LIMIT = 56 * 1024 * 1024

TQ = 128
ROWS = NSA_HPG * TQ
SEL_TK = 512
DIAG_W = 2 * TQ
WIN_W = WINDOW + TQ
HEAD_PERM = tuple(g * NSA_HPG + j for j in range(NSA_HPG) for g in range(NSA_KV_GROUPS))


def _dot(a, b):
    return jnp.dot(a, b, preferred_element_type=F32)


def _dot_nt(a, b):
    return lax.dot_general(a, b, (((1,), (1,)), ((), ())), preferred_element_type=F32)


def _dot_tn(a, b):
    return lax.dot_general(a, b, (((0,), (0,)), ((), ())), preferred_element_type=F32)


def _dot_split(x, w, parts, x_is_rhs=False):
    acc = None
    r = x
    for p in range(parts):
        h = r.astype(MXU_DT)
        d = _dot(w, h) if x_is_rhs else _dot(h, w)
        acc = d if acc is None else acc + d
        if p + 1 < parts:
            r = r - h.astype(F32)
    return acc


def _cparams(sem):
    return pltpu.CompilerParams(dimension_semantics=sem, vmem_limit_bytes=VMEM_LIMIT)


def _inproj_kernel(x_ref, wh_ref, wn_ref, wc_ref, wg_ref, oh_ref, on_ref, okc_ref, ovc_ref, og_ref):
    xb = x_ref[...].astype(MXU_DT)
    nh = oh_ref.shape[1]
    for c in range(0, nh, 512):
        oh_ref[:, c:c + 512] = _dot(xb, wh_ref[:, c:c + 512])
    nn = on_ref.shape[1]
    for c in range(0, nn, 512):
        w = min(512, nn - c)
        on_ref[:, c:c + w] = _dot(xb, wn_ref[:, c:c + w]).astype(on_ref.dtype)
    kv = _dot(xb, wc_ref[...])
    okc_ref[...] = kv[:, :KV_W].astype(okc_ref.dtype)
    ovc_ref[...] = kv[:, KV_W:].astype(ovc_ref.dtype)
    og_ref[...] = _dot(xb, wg_ref[...])


def _inproj(x2, wh, wn, wc, wg, tm=512):
    T = x2.shape[0]
    nh, nn = wh.shape[1], wn.shape[1]
    full = lambda i: (0, 0)
    row = lambda i: (i, 0)
    return pl.pallas_call(
        _inproj_kernel,
        grid=(T // tm,),
        in_specs=[pl.BlockSpec((tm, D_MODEL), row),
                  pl.BlockSpec((D_MODEL, nh), full),
                  pl.BlockSpec((D_MODEL, nn), full),
                  pl.BlockSpec((D_MODEL, 2 * KV_W), full),
                  pl.BlockSpec((D_MODEL, LANES), full)],
        out_specs=[pl.BlockSpec((tm, nh), row),
                   pl.BlockSpec((tm, nn), row),
                   pl.BlockSpec((tm, KV_W), row),
                   pl.BlockSpec((tm, KV_W), row),
                   pl.BlockSpec((tm, LANES), row)],
        out_shape=[jax.ShapeDtypeStruct((T, nh), F32),
                   jax.ShapeDtypeStruct((T, nn), MXU_DT),
                   jax.ShapeDtypeStruct((T, KV_W), MXU_DT),
                   jax.ShapeDtypeStruct((T, KV_W), MXU_DT),
                   jax.ShapeDtypeStruct((T, LANES), F32)],
        compiler_params=_cparams(("parallel",)),
        name="inproj",
    )(x2, wh, wn, wc, wg)


_HG_LEVELS = (32, 16, 8, 4, 2, 1)
_HG_NLEV = len(_HG_LEVELS)


def _hgrn_consts():
    C = HG_CHUNK
    a = np.zeros((8 * C, C), np.float32)
    m = np.zeros((_HG_NLEV + 1, C, C), np.float32)
    for l, h in enumerate(_HG_LEVELS):
        for r in range(C):
            mid = (r // (2 * h)) * 2 * h + h
            if r >= mid:
                a[l * C + r, mid:r + 1] = 1.0
                m[l, r, mid - h:mid] = 1.0
            else:
                a[l * C + r, r + 1:mid] = -1.0
    for r in range(C):
        a[6 * C + r, :r + 1] = 1.0
        a[7 * C + r, r + 1:] = 1.0
        m[_HG_NLEV, r, r] = 1.0
    return a, m


def _hgrn_kernel(q_ref, f_ref, i_ref, g_ref, loglb_ref, log1mlb_ref, onemlb_ref, gw_ref, amat_ref, mask_ref,
                 o_ref, state_ref, logf_s, k_s):
    C, Dk = HG_CHUNK, HG_HEAD_DIM
    ts = q_ref.shape[0]

    @pl.when(pl.program_id(1) == 0)
    def _():
        state_ref[...] = jnp.zeros_like(state_ref)

    z = f_ref[...]
    e = jnp.exp(-jnp.abs(z))
    logsig = jnp.minimum(z, 0.0) - jnp.log1p(e)
    a = loglb_ref[...]
    bt = log1mlb_ref[...] + logsig
    logf_s[...] = jnp.maximum(a, bt) + jnp.log1p(jnp.exp(-jnp.abs(a - bt)))
    k_s[...] = onemlb_ref[...] * jnp.where(z > 0.0, e, 1.0) / (1.0 + e)

    amat = amat_ref[...]

    def chunk(c, carry):
        r0 = pl.multiple_of(c * C, C)
        rows = pl.ds(r0, C)
        for h in range(HG_HEADS):
            cols = slice(h * Dk, (h + 1) * Dk)
            lf = logf_s[rows, cols]
            d = _dot_split(lf, amat, 2, x_is_rhs=True)
            q = q_ref[rows, cols]
            k = k_s[rows, cols]
            v = i_ref[rows, cols].astype(MXU_DT)
            sc = _dot_nt(q.astype(MXU_DT), k.astype(MXU_DT)) * mask_ref[_HG_NLEV]
            for l in range(_HG_NLEV):
                el = jnp.exp(-jnp.abs(d[l * C:(l + 1) * C]))
                sc = sc + _dot_nt((q * el).astype(MXU_DT), (k * el).astype(MXU_DT)) * mask_ref[l]
            b = d[6 * C:7 * C]
            b_last = d[7 * C - 1:7 * C]
            st = state_ref[h]
            o = _dot(sc.astype(MXU_DT), v) + _dot_nt((q * jnp.exp(b)).astype(MXU_DT), st.astype(MXU_DT))
            kdec = (k * jnp.exp(d[7 * C:8 * C])).astype(MXU_DT)
            state_ref[h] = st * jnp.exp(b_last) + _dot_tn(v, kdec)
            o = o * lax.rsqrt(jnp.mean(o * o, axis=-1, keepdims=True) + 1e-6) * gw_ref[...]
            g = g_ref[rows, cols]
            o_ref[rows, cols] = (o * (g / (1.0 + jnp.exp(-g)))).astype(o_ref.dtype)
        return carry

    lax.fori_loop(0, ts // C, chunk, 0)


def _hgrn(proj_h, loglb, log1mlb, onemlb, gw, amat, masks, B, S, ts=512):
    nt = S // ts
    sec = lambda s: (lambda b, t: (b * nt + t, s))
    vec = pl.BlockSpec((1, HG_WIDTH), lambda b, t: (0, 0))
    return pl.pallas_call(
        _hgrn_kernel,
        grid=(B, nt),
        in_specs=[pl.BlockSpec((ts, HG_WIDTH), sec(0)),
                  pl.BlockSpec((ts, HG_WIDTH), sec(1)),
                  pl.BlockSpec((ts, HG_WIDTH), sec(2)),
                  pl.BlockSpec((ts, HG_WIDTH), sec(3)),
                  vec, vec, vec,
                  pl.BlockSpec((1, HG_HEAD_DIM), lambda b, t: (0, 0)),
                  pl.BlockSpec(amat.shape, lambda b, t: (0, 0)),
                  pl.BlockSpec(masks.shape, lambda b, t: (0, 0, 0))],
        out_specs=pl.BlockSpec((ts, HG_WIDTH), lambda b, t: (b * nt + t, 0)),
        out_shape=jax.ShapeDtypeStruct((B * S, HG_WIDTH), MXU_DT),
        scratch_shapes=[pltpu.VMEM((HG_HEADS, HG_HEAD_DIM, HG_HEAD_DIM), F32),
                        pltpu.VMEM((ts, HG_WIDTH), F32),
                        pltpu.VMEM((ts, HG_WIDTH), F32)],
        compiler_params=_cparams(("arbitrary", "arbitrary")),
        name="hgrn2",
    )(proj_h, proj_h, proj_h, proj_h, loglb, log1mlb, onemlb, gw, amat, masks)


def _gelu_tanh(x):
    return 0.5 * x * (1.0 + jnp.tanh(math.sqrt(2.0 / math.pi) * (x + 0.044715 * (x * x * x))))


def _compress_kernel(kc_ref, vc_ref, pka_ref, pkb_ref, wka_ref, wkb_ref, wk2_ref,
                     pva_ref, pvb_ref, wva_ref, wvb_ref, wv2_ref, ok_ref, ov_ref):
    def one(x_ref, pa_ref, pb_ref, wa_ref, wb_ref, w2_ref, o_ref):
        x = x_ref[...]
        n = x.shape[0]
        ha = _dot(x, wa_ref[...])
        hb = _dot(x, wb_ref[...])
        posw = _dot(pa_ref[...], wa_ref[...]) + _dot(pb_ref[...], wb_ref[...])
        h = ha + pltpu.roll(hb, n - 1, axis=0) + posw[0:1]
        o_ref[...] = _dot(_gelu_tanh(h).astype(MXU_DT), w2_ref[...]).astype(o_ref.dtype)

    one(kc_ref, pka_ref, pkb_ref, wka_ref, wkb_ref, wk2_ref, ok_ref)
    one(vc_ref, pva_ref, pvb_ref, wva_ref, wvb_ref, wv2_ref, ov_ref)


def _compress(kc3, vc3, kparams, vparams):
    B, n, w = kc3.shape
    x_spec = pl.BlockSpec((None, n, w), lambda b: (b, 0, 0))
    full = lambda a: pl.BlockSpec(a.shape, lambda b: (0,) * a.ndim)
    o_spec = pl.BlockSpec((None, n, KV_W), lambda b: (b, 0, 0))
    o_shape = jax.ShapeDtypeStruct((B, n, KV_W), MXU_DT)
    return pl.pallas_call(
        _compress_kernel,
        grid=(B,),
        in_specs=[x_spec, x_spec] + [full(a) for a in kparams] + [full(a) for a in vparams],
        out_specs=[o_spec, o_spec],
        out_shape=[o_shape, o_shape],
        compiler_params=_cparams(("parallel",)),
        name="nsa_compress",
    )(kc3, vc3, *kparams, *vparams)


def _compress_params(pos, w1, w2):
    G, Dh, Hd = NSA_KV_GROUPS, NSA_HEAD_DIM, CMP_HIDDEN
    eye = jnp.eye(G, dtype=F32)

    def half(wh, ph):
        wbig = jnp.einsum('rdj,gk->rgdkj', wh.reshape(CMP_STRIDE, Dh, Hd), eye).reshape(CMP_STRIDE * G * Dh, G * Hd)
        pbig = jnp.broadcast_to(ph[:, None, :], (CMP_STRIDE, G, Dh)).reshape(1, CMP_STRIDE * G * Dh)
        return jnp.broadcast_to(pbig, (8, pbig.shape[1])).astype(MXU_DT), wbig.astype(MXU_DT)

    pa, wa = half(w1[:CMP_STRIDE * Dh], pos[:CMP_STRIDE])
    pb, wb = half(w1[CMP_STRIDE * Dh:], pos[CMP_STRIDE:])
    w2big = jnp.einsum('jd,gk->gjkd', w2, eye).reshape(G * Hd, G * Dh).astype(MXU_DT)
    return pa, pb, wa, wb, w2big


def _t5_bucket_np(n):
    n = np.maximum(n, 0)
    max_exact = N_BUCKETS // 2
    nf = np.maximum(n, 1).astype(np.float32)
    large = max_exact + (np.log(nf / np.float32(max_exact)) / np.float32(math.log(MAX_DISTANCE / max_exact))
                         * np.float32(N_BUCKETS - max_exact)).astype(np.int32)
    large = np.minimum(large, N_BUCKETS - 1)
    return np.where(n < max_exact, n, large).astype(np.int32)


def _nsa_tables(rel_bias, S):
    rb = rel_bias.astype(F32)
    heads = np.array([[g * NSA_HPG + j for j in range(NSA_HPG)] for g in range(NSA_KV_GROUPS)])

    def table(dist, valid, rel_to_far):
        b = rb[jnp.asarray(_t5_bucket_np(dist))]
        if rel_to_far:
            b = b - rb[N_BUCKETS - 1]
        b = jnp.moveaxis(b, -1, 0)[jnp.asarray(heads)]
        return jnp.where(jnp.asarray(valid), b, MASK_NEG)

    i = np.arange(TQ)[:, None]
    d_sel = i + TQ - np.arange(DIAG_W)[None, :]
    t_sel = table(d_sel, d_sel >= 0, True).reshape(NSA_KV_GROUPS, ROWS, DIAG_W)
    d_win = i + WINDOW - np.arange(WIN_W)[None, :]
    t_win = table(d_win, (d_win >= 0) & (d_win < WINDOW), False).reshape(NSA_KV_GROUPS, ROWS, WIN_W)
    n_cmp_pad = S // CMP_STRIDE
    t = np.arange(S)[:, None]
    cmp_end = np.arange(n_cmp_pad)[None, :] * CMP_STRIDE + CMP_BLOCK - 1
    d_cmp = t - cmp_end
    t_cmp = table(d_cmp, d_cmp >= 0, False)
    t_cmp = t_cmp.reshape(NSA_KV_GROUPS, NSA_HPG, S // TQ, TQ, n_cmp_pad).transpose(2, 0, 1, 3, 4)
    t_cmp = t_cmp.reshape(S // TQ, NSA_KV_GROUPS, ROWS, n_cmp_pad)
    return t_sel, t_win, t_cmp


def _nsa_consts(S):
    n_slc = S // SLC_BLOCK
    n_cmp_pad = S // CMP_STRIDE
    assert n_slc <= LANES
    eind = np.zeros((S, LANES), np.float32)
    eind[np.arange(S), np.arange(S) // SLC_BLOCK] = 1.0
    cs = np.arange(n_cmp_pad)[:, None] * CMP_STRIDE
    ss = np.arange(LANES)[None, :] * SLC_BLOCK
    ov = np.clip(np.minimum(cs + CMP_BLOCK, ss + SLC_BLOCK) - np.maximum(cs, ss), 0, None)
    agg = (ov / CMP_STRIDE).astype(np.float32)
    agg[n_cmp_pad - 1] = 0.0
    agg[:, n_slc:] = 0.0
    ex = np.zeros((LANES, 3 * NSA_HPG * LANES), np.float32)
    for br in range(3):
        for j in range(NSA_HPG):
            c0 = (br * NSA_HPG + j) * LANES
            ex[br * NSA_HEADS + j, c0:c0 + NSA_HEAD_DIM] = 1.0
            ex[br * NSA_HEADS + NSA_HPG + j, c0 + NSA_HEAD_DIM:c0 + LANES] = 1.0
    return eind, agg, ex


def _softmax_update(s, v, m_ref, l_ref, acc_ref):
    m_prev = m_ref[...]
    m_new = jnp.maximum(m_prev, jnp.max(s, axis=-1, keepdims=True))
    alpha = jnp.exp(m_prev - m_new)
    p = jnp.exp(s - m_new)
    l_ref[...] = alpha * l_ref[...] + jnp.sum(p, axis=-1, keepdims=True)
    acc_ref[...] = alpha * acc_ref[...] + _dot(p.astype(MXU_DT), v)
    m_ref[...] = m_new


def _nsa_kernel(q_ref, gt_ref, kc_ref, vc_ref, ks_ref, vs_ref, kw_ref, vw_ref,
                tsel_ref, twin_ref, tcmp_ref, eind_ref, agg_ref, ex_ref,
                o_ref, kaug_s, vsel_s, kwin_s, vwin_s, m_s, l_s, acc_s):
    S = ks_ref.shape[0]
    n_slc = S // SLC_BLOCK
    qt = pl.program_id(1)
    t0 = pl.multiple_of(qt * TQ, TQ)

    @pl.when(qt == 0)
    def _():
        kaug_s[0:TQ, :] = jnp.zeros((TQ, 2 * LANES), MXU_DT)
        kaug_s[TQ:, 0:LANES] = ks_ref[...]
        kaug_s[TQ:, LANES:] = eind_ref[...]
        vsel_s[0:TQ, :] = jnp.zeros((TQ, LANES), MXU_DT)
        vsel_s[TQ:, :] = vs_ref[...]
        kwin_s[0:WINDOW, :] = jnp.zeros((WINDOW, LANES), MXU_DT)
        kwin_s[WINDOW:, :] = kw_ref[...]
        vwin_s[0:WINDOW, :] = jnp.zeros((WINDOW, LANES), MXU_DT)
        vwin_s[WINDOW:, :] = vw_ref[...]

    lane = lax.broadcasted_iota(jnp.int32, (1, LANES), 1)
    lo_half = lane < NSA_HEAD_DIM
    scale = NSA_HEAD_DIM ** -0.5
    zero = jnp.zeros((), MXU_DT)

    gsig = 1.0 / (1.0 + jnp.exp(-gt_ref[...]))
    gexp = _dot_split(gsig, ex_ref[...], 2)

    trow = t0 + lax.broadcasted_iota(jnp.int32, (TQ, 1), 0)
    cur = trow // SLC_BLOCK
    lane_f = lane.astype(F32)

    outs = []
    for g in range(NSA_KV_GROUPS):
        mine = lo_half if g == 0 else jnp.logical_not(lo_half)
        qg = jnp.concatenate(
            [jnp.where(mine, (q_ref[:, j * LANES:(j + 1) * LANES].astype(F32) * scale).astype(MXU_DT), zero)
             for j in range(NSA_HPG)], axis=0)

        tc = tcmp_ref[g]
        valid_c = tc > 0.5 * MASK_NEG
        s_c = _dot_nt(qg, kc_ref[...]) + tc
        m_c = jnp.max(s_c, axis=-1, keepdims=True)
        p_c = jnp.where(valid_c, jnp.exp(s_c - m_c), 0.0)
        l_c = jnp.sum(p_c, axis=-1, keepdims=True)
        p_c = p_c / jnp.where(l_c > 0.0, l_c, 1.0)
        o_c = _dot(p_c.astype(MXU_DT), vc_ref[...])

        p_sum = p_c[0:TQ]
        for j in range(1, NSA_HPG):
            p_sum = p_sum + p_c[j * TQ:(j + 1) * TQ]
        imp = _dot_split(p_sum, agg_ref[...], 3)
        diff = cur - lane
        forced = (lane == 0) | ((diff >= 0) & (diff < SLC_LOCAL))
        causal_blk = lane * SLC_BLOCK <= trow
        score = jnp.where(forced, BIG, jnp.where(causal_blk, imp, -BIG))
        score = jnp.where(lane < n_slc, score, -jnp.inf)
        sel = jnp.zeros((TQ, LANES), F32)
        for _ in range(min(SLC_TOPK, n_slc)):
            mx = jnp.max(score, axis=-1, keepdims=True)
            idx = jnp.min(jnp.where(score == mx, lane_f, float(LANES)), axis=-1, keepdims=True)
            hit = lane_f == idx
            sel = jnp.where(hit, 1.0, sel)
            score = jnp.where(hit, -jnp.inf, score)
        selb = ((sel - 1.0) * (-SEL_NEG)).astype(MXU_DT)
        qa = jnp.concatenate([qg, jnp.concatenate([selb] * NSA_HPG, axis=0)], axis=1)

        m_s[...] = jnp.full(m_s.shape, MASK_NEG, F32)
        l_s[...] = jnp.zeros(l_s.shape, F32)
        acc_s[...] = jnp.zeros(acc_s.shape, F32)
        n_far = jnp.maximum(t0 - TQ, 0)

        def far_tile(width):
            def body(i, start0):
                start = pl.multiple_of(start0 + i * width, LANES)
                rows = pl.ds(TQ + start, width)
                _softmax_update(_dot_nt(qa, kaug_s[rows, :]), vsel_s[rows, :], m_s, l_s, acc_s)
                return start0
            return body

        n_full = n_far // SEL_TK
        lax.fori_loop(0, n_full, far_tile(SEL_TK), 0)
        lax.fori_loop(0, (n_far - n_full * SEL_TK) // LANES, far_tile(LANES), n_full * SEL_TK)

        rows = pl.ds(t0, DIAG_W)
        kpos = t0 - TQ + lax.broadcasted_iota(jnp.int32, (1, DIAG_W), 1)
        s_d = jnp.where(kpos >= 0, _dot_nt(qa, kaug_s[rows, :]) + tsel_ref[g], MASK_NEG)
        _softmax_update(s_d, vsel_s[rows, :], m_s, l_s, acc_s)
        o_s = acc_s[...] / l_s[...]

        rows = pl.ds(t0, WIN_W)
        kpos = t0 - WINDOW + lax.broadcasted_iota(jnp.int32, (1, WIN_W), 1)
        s_w = jnp.where(kpos >= 0, _dot_nt(qg, kwin_s[rows, :]) + twin_ref[g], MASK_NEG)
        m_w = jnp.max(s_w, axis=-1, keepdims=True)
        p_w = jnp.exp(s_w - m_w)
        o_w = _dot(p_w.astype(MXU_DT), vwin_s[rows, :]) / jnp.sum(p_w, axis=-1, keepdims=True)

        outs.append((o_c, o_s, o_w))

    for j in range(NSA_HPG):
        r = slice(j * TQ, (j + 1) * TQ)
        acc = None
        for br in range(3):
            gate = gexp[:, (br * NSA_HPG + j) * LANES:(br * NSA_HPG + j + 1) * LANES]
            term = gate * jnp.where(lo_half, outs[0][br][r], outs[1][br][r])
            acc = term if acc is None else acc + term
        o_ref[:, j * LANES:(j + 1) * LANES] = acc.astype(o_ref.dtype)


def _nsa(proj_n, gates, kcmp, vcmp, tsel, twin, tcmp, eind, agg, ex, B, S):
    nq = S // TQ
    n_cmp_pad = S // CMP_STRIDE
    qcols = NSA_WIDTH // LANES
    tok = lambda b, t: (b * nq + t, 0)
    kv = lambda c: pl.BlockSpec((S, KV_W), lambda b, t: (b, qcols + c))
    const = lambda a: pl.BlockSpec(a.shape, lambda b, t: (0,) * a.ndim)
    return pl.pallas_call(
        _nsa_kernel,
        grid=(B, nq),
        in_specs=[pl.BlockSpec((TQ, NSA_WIDTH), tok),
                  pl.BlockSpec((TQ, LANES), tok),
                  pl.BlockSpec((None, n_cmp_pad, KV_W), lambda b, t: (b, 0, 0)),
                  pl.BlockSpec((None, n_cmp_pad, KV_W), lambda b, t: (b, 0, 0)),
                  kv(0), kv(1), kv(2), kv(3),
                  const(tsel), const(twin),
                  pl.BlockSpec((None, NSA_KV_GROUPS, ROWS, n_cmp_pad), lambda b, t: (t, 0, 0, 0)),
                  const(eind), const(agg), const(ex)],
        out_specs=pl.BlockSpec((TQ, NSA_WIDTH), tok),
        out_shape=jax.ShapeDtypeStruct((B * S, NSA_WIDTH), MXU_DT),
        scratch_shapes=[pltpu.VMEM((TQ + S, 2 * LANES), MXU_DT),
                        pltpu.VMEM((TQ + S, LANES), MXU_DT),
                        pltpu.VMEM((WINDOW + S, LANES), MXU_DT),
                        pltpu.VMEM((WINDOW + S, LANES), MXU_DT),
                        pltpu.VMEM((ROWS, 1), F32),
                        pltpu.VMEM((ROWS, 1), F32),
                        pltpu.VMEM((ROWS, LANES), F32)],
        compiler_params=_cparams(("arbitrary", "arbitrary")),
        name="nsa_attention",
    )(proj_n, gates, kcmp, vcmp, proj_n, proj_n, proj_n, proj_n, tsel, twin, tcmp, eind, agg, ex)


def _layer_norm(x, g, b):
    mu = jnp.mean(x, axis=-1, keepdims=True)
    xc = x - mu
    var = jnp.mean(xc * xc, axis=-1, keepdims=True)
    return xc * lax.rsqrt(var + 1e-5) * g + b


def _post_kernel(x_ref, oa_ref, ob_ref, woa_ref, wob_ref, g1_ref, b1_ref, wg_ref, wu_ref, wd_ref, g2_ref, b2_ref,
                 o_ref, acc_ref):
    mix = _dot(oa_ref[...], woa_ref[...]) + _dot(ob_ref[...], wob_ref[...])
    x1 = _layer_norm(ALPHA * x_ref[...] + mix, g1_ref[...], b1_ref[...])
    xb = x1.astype(MXU_DT)
    tf = 256
    for c in range(0, D_FF, tf):
        gate = _dot(xb, wg_ref[:, c:c + tf])
        up = _dot(xb, wu_ref[:, c:c + tf])
        h = (gate / (1.0 + jnp.exp(-gate)) * up).astype(MXU_DT)
        d = _dot(h, wd_ref[c:c + tf, :])
        if c == 0:
            acc_ref[...] = d
        else:
            acc_ref[...] += d
    o_ref[...] = _layer_norm(ALPHA * x1 + acc_ref[...], g2_ref[...], b2_ref[...])


def _post(x2, oa, ob, woa, wob, g1, b1, wg, wu, wd, g2, b2, tm=256):
    T = x2.shape[0]
    row = lambda i: (i, 0)
    const = lambda a: pl.BlockSpec(a.shape, lambda i: (0, 0), pipeline_mode=pl.Buffered(1))
    return pl.pallas_call(
        _post_kernel,
        grid=(T // tm,),
        in_specs=[pl.BlockSpec((tm, D_MODEL), row),
                  pl.BlockSpec((tm, HG_WIDTH), row),
                  pl.BlockSpec((tm, NSA_WIDTH), row),
                  const(woa), const(wob), const(g1), const(b1),
                  const(wg), const(wu), const(wd), const(g2), const(b2)],
        out_specs=pl.BlockSpec((tm, D_MODEL), row),
        out_shape=jax.ShapeDtypeStruct((T, D_MODEL), F32),
        scratch_shapes=[pltpu.VMEM((tm, D_MODEL), F32)],
        compiler_params=_cparams(("parallel",)),
        name="outproj_ffn",
    )(x2, oa, ob, woa, wob, g1, b1, wg, wu, wd, g2, b2)


def _split_w_in(w):
    o = 4 * HG_WIDTH
    wh = w[:, :o]
    wq = w[:, o:o + NSA_WIDTH].reshape(D_MODEL, NSA_HEADS, NSA_HEAD_DIM)[:, list(HEAD_PERM)].reshape(D_MODEL, NSA_WIDTH)
    o += NSA_WIDTH
    kc, vc, ks, vs, kw, vw = (w[:, o + i * KV_W:o + (i + 1) * KV_W] for i in range(6))
    o += 6 * KV_W
    wgt = jnp.pad(w[:, o:], ((0, 0), (0, LANES - 3 * NSA_HEADS)))
    wn = jnp.concatenate([wq, ks, vs, kw, vw], axis=1)
    wc = jnp.concatenate([kc, vc], axis=1)
    return wh.astype(MXU_DT), wn.astype(MXU_DT), wc.astype(MXU_DT), wgt.astype(MXU_DT)


def kernel(x, w_in, hg_lb_param, hg_norm_w, cmp_pos_k, cmp_w1_k, cmp_w2_k, cmp_pos_v, cmp_w1_v, cmp_w2_v,
           rel_bias, w_out, ln1_g, ln1_b, w_ffn_gate, w_ffn_up, w_ffn_down, ln2_g, ln2_b):
    B, S, _ = x.shape
    T = B * S
    depth = w_in.shape[0]

    p = jax.nn.softmax(hg_lb_param.astype(F32), axis=0)
    c = jnp.cumsum(p, axis=0)
    lbs = c - c[0:1]
    loglb, log1mlb, onemlb = jnp.log(lbs), jnp.log1p(-lbs), 1.0 - lbs

    amat_np, masks_np = _hgrn_consts()
    amat, masks = jnp.asarray(amat_np, MXU_DT), jnp.asarray(masks_np)
    eind_np, agg_np, ex_np = _nsa_consts(S)
    eind, agg, ex = jnp.asarray(eind_np, MXU_DT), jnp.asarray(agg_np, MXU_DT), jnp.asarray(ex_np, MXU_DT)
    tsel, twin, tcmp = _nsa_tables(rel_bias, S)
    out_perm = np.concatenate([np.arange(h * NSA_HEAD_DIM, (h + 1) * NSA_HEAD_DIM) for h in HEAD_PERM])

    x2 = x.reshape(T, D_MODEL).astype(F32)
    for l in range(depth):
        wh, wn, wc, wgt = _split_w_in(w_in[l])
        proj_h, proj_n, kc, vc, gates = _inproj(x2, wh, wn, wc, wgt)
        o_a = _hgrn(proj_h, loglb[l:l + 1], log1mlb[l:l + 1], onemlb[l:l + 1], hg_norm_w[l][None, :], amat, masks, B, S)
        shape3 = (B, S // CMP_STRIDE, CMP_STRIDE * KV_W)
        kcmp, vcmp = _compress(kc.reshape(shape3), vc.reshape(shape3),
                               _compress_params(cmp_pos_k[l], cmp_w1_k[l], cmp_w2_k[l]),
                               _compress_params(cmp_pos_v[l], cmp_w1_v[l], cmp_w2_v[l]))
        o_b = _nsa(proj_n, gates, kcmp, vcmp, tsel, twin, tcmp, eind, agg, ex, B, S)
        woa = w_out[l][:HG_WIDTH].astype(MXU_DT)
        wob = w_out[l][HG_WIDTH:][out_perm].astype(MXU_DT)
        x2 = _post(x2, o_a, o_b, woa, wob, ln1_g[l][None, :], ln1_b[l][None, :],
                   w_ffn_gate[l].astype(MXU_DT), w_ffn_up[l].astype(MXU_DT), w_ffn_down[l].astype(MXU_DT),
                   ln2_g[l][None, :], ln2_b[l][None, :])
    return x2.reshape(B, S, D_MODEL).astype(x.dtype)
```

```python
import math

import numpy as np
import jax
import jax.numpy as jnp
from jax import lax
from jax.experimental import pallas as pl
from jax.experimental.pallas import tpu as pltpu

F32 = jnp.float32
MXU_DT = jnp.bfloat16

D_MODEL = 1024
DEPTH = 4
HG_WIDTH = 512
HG_HEAD_DIM = 128
HG_HEADS = 4
HG_CHUNK = 64
NSA_WIDTH = 512
NSA_HEAD_DIM = 64
NSA_HEADS = 8
NSA_KV_GROUPS = 2
NSA_HPG = 4
CMP_BLOCK = 32
CMP_STRIDE = 16
CMP_HIDDEN = 128
SLC_BLOCK = 64
SLC_TOPK = 16
SLC_LOCAL = 2
WINDOW = 512
N_BUCKETS = 32
MAX_DISTANCE = 128
D_FF = 2816
ALPHA = (2.0 * DEPTH) ** 0.25
KV_W = NSA_KV_GROUPS * NSA_HEAD_DIM
LANES = 128
SUBLANES = 8
BIG = 1e9
MASK_NEG = -1e30
SEL_NEG = MASK_NEG
LOG2E = math.log2(math.e)

VMEM_LIMIT = 56 * 1024 * 1024

TQ = 128
COLS = NSA_HEADS * TQ
SEL_TK = 512
SEL_SUB = 256
DIAG_W = 2 * TQ
WIN_W = WINDOW + TQ
CMP_BAND = 24
CMP_TAB = 40
HEAD_PERM = tuple(g * NSA_HPG + j for j in range(NSA_HPG) for g in range(NSA_KV_GROUPS))


def _dot(a, b):
    return jnp.dot(a, b, preferred_element_type=F32)


def _dot_nt(a, b):
    return lax.dot_general(a, b, (((1,), (1,)), ((), ())), preferred_element_type=F32)


def _dot_tn(a, b):
    return lax.dot_general(a, b, (((0,), (0,)), ((), ())), preferred_element_type=F32)


def _dot_split(x, w, parts, x_is_rhs=False):
    acc = None
    r = x
    for p in range(parts):
        h = r.astype(MXU_DT)
        d = _dot(w, h) if x_is_rhs else _dot(h, w)
        acc = d if acc is None else acc + d
        if p + 1 < parts:
            r = r - h.astype(F32)
    return acc


def _cparams(sem):
    return pltpu.CompilerParams(dimension_semantics=sem, vmem_limit_bytes=VMEM_LIMIT)


def _inproj_kernel(x_ref, wh_ref, wn_ref, wc_ref, wg_ref, oh_ref, on_ref, okc_ref, ovc_ref, og_ref):
    xb = x_ref[...].astype(MXU_DT)
    nh = oh_ref.shape[1]
    for c in range(0, nh, 512):
        oh_ref[:, c:c + 512] = _dot(xb, wh_ref[:, c:c + 512])
    nn = on_ref.shape[1]
    for c in range(0, nn, 512):
        w = min(512, nn - c)
        on_ref[:, c:c + w] = _dot(xb, wn_ref[:, c:c + w]).astype(on_ref.dtype)
    kv = _dot(xb, wc_ref[...])
    okc_ref[...] = kv[:, :KV_W].astype(okc_ref.dtype)
    ovc_ref[...] = kv[:, KV_W:].astype(ovc_ref.dtype)
    og_ref[...] = _dot(xb, wg_ref[...])


def _inproj(x2, wh, wn, wc, wg, tm=512):
    T = x2.shape[0]
    nh, nn = wh.shape[1], wn.shape[1]
    full = lambda i: (0, 0)
    row = lambda i: (i, 0)
    return pl.pallas_call(
        _inproj_kernel,
        grid=(T // tm,),
        in_specs=[pl.BlockSpec((tm, D_MODEL), row),
                  pl.BlockSpec((D_MODEL, nh), full),
                  pl.BlockSpec((D_MODEL, nn), full),
                  pl.BlockSpec((D_MODEL, 2 * KV_W), full),
                  pl.BlockSpec((D_MODEL, LANES), full)],
        out_specs=[pl.BlockSpec((tm, nh), row),
                   pl.BlockSpec((tm, nn), row),
                   pl.BlockSpec((tm, KV_W), row),
                   pl.BlockSpec((tm, KV_W), row),
                   pl.BlockSpec((tm, LANES), row)],
        out_shape=[jax.ShapeDtypeStruct((T, nh), F32),
                   jax.ShapeDtypeStruct((T, nn), MXU_DT),
                   jax.ShapeDtypeStruct((T, KV_W), MXU_DT),
                   jax.ShapeDtypeStruct((T, KV_W), MXU_DT),
                   jax.ShapeDtypeStruct((T, LANES), F32)],
        compiler_params=_cparams(("parallel",)),
        name="inproj",
    )(x2, wh, wn, wc, wg)


_HG_LEVELS = (32, 16, 8, 4, 2, 1)
_HG_NLEV = len(_HG_LEVELS)
HG_FAST_LIMIT = -80.0


def _hgrn_consts():
    C = HG_CHUNK
    a = np.zeros((8 * C, C), np.float32)
    m = np.zeros((_HG_NLEV + 1, C, C), np.float32)
    for l, h in enumerate(_HG_LEVELS):
        for r in range(C):
            mid = (r // (2 * h)) * 2 * h + h
            if r >= mid:
                a[l * C + r, mid:r + 1] = 1.0
                m[l, r, mid - h:mid] = 1.0
            else:
                a[l * C + r, r + 1:mid] = -1.0
    for r in range(C):
        a[6 * C + r, :r + 1] = 1.0
        a[7 * C + r, r + 1:] = 1.0
        m[_HG_NLEV, r, r] = 1.0
    return a, m


def _hgrn_kernel(q_ref, f_ref, i_ref, g_ref, loglb_ref, log1mlb_ref, onemlb_ref, gw_ref, amat_ref, mask_ref,
                 o_ref, state_ref, logf_s, k_s):
    C, Dk = HG_CHUNK, HG_HEAD_DIM
    ts = q_ref.shape[0]

    @pl.when(pl.program_id(1) == 0)
    def _():
        state_ref[...] = jnp.zeros_like(state_ref)

    z = f_ref[...]
    e = jnp.exp(-jnp.abs(z))
    logsig = jnp.minimum(z, 0.0) - jnp.log1p(e)
    a = loglb_ref[...]
    bt = log1mlb_ref[...] + logsig
    logf_s[...] = jnp.maximum(a, bt) + jnp.log1p(jnp.exp(-jnp.abs(a - bt)))
    k_s[...] = onemlb_ref[...] * jnp.where(z > 0.0, e, 1.0) / (1.0 + e)

    def finish(o, rows, cols):
        o = o * lax.rsqrt(jnp.mean(o * o, axis=-1, keepdims=True) + 1e-6) * gw_ref[...]
        g = g_ref[rows, cols]
        o_ref[rows, cols] = (o * (g / (1.0 + jnp.exp(-g)))).astype(o_ref.dtype)

    def robust():
        amat = amat_ref[...]

        def chunk(c, carry):
            r0 = pl.multiple_of(c * C, C)
            rows = pl.ds(r0, C)
            for h in range(HG_HEADS):
                cols = slice(h * Dk, (h + 1) * Dk)
                d = _dot_split(logf_s[rows, cols], amat, 2, x_is_rhs=True)
                q = q_ref[rows, cols]
                k = k_s[rows, cols]
                v = i_ref[rows, cols].astype(MXU_DT)
                sc = _dot_nt(q.astype(MXU_DT), k.astype(MXU_DT)) * mask_ref[_HG_NLEV]
                for l in range(_HG_NLEV):
                    el = jnp.exp(-jnp.abs(d[l * C:(l + 1) * C]))
                    sc = sc + _dot_nt((q * el).astype(MXU_DT), (k * el).astype(MXU_DT)) * mask_ref[l]
                b = d[6 * C:7 * C]
                st = state_ref[h]
                o = _dot(sc.astype(MXU_DT), v) + _dot_nt((q * jnp.exp(b)).astype(MXU_DT), st.astype(MXU_DT))
                kdec = (k * jnp.exp(d[7 * C:8 * C])).astype(MXU_DT)
                state_ref[h] = st * jnp.exp(b[C - 1:C]) + _dot_tn(v, kdec)
                finish(o, rows, cols)
            return carry

        lax.fori_loop(0, ts // C, chunk, 0)

    def fast():
        tril_mat = amat_ref[6 * C:7 * C, :]
        causal = lax.broadcasted_iota(jnp.int32, (C, C), 0) >= lax.broadcasted_iota(jnp.int32, (C, C), 1)
        states = [state_ref[h] for h in range(HG_HEADS)]
        for c in range(ts // C):
            rows = slice(c * C, (c + 1) * C)
            b = _dot_split(logf_s[rows, :], tril_mat, 2, x_is_rhs=True)
            e = jnp.exp(b)
            dec = e[C - 1:C]
            qe = (q_ref[rows, :] * e).astype(MXU_DT)
            kt = k_s[rows, :] * jnp.exp(-b)
            ktb = kt.astype(MXU_DT)
            kdec = (kt * dec).astype(MXU_DT)
            for h in range(HG_HEADS):
                cols = slice(h * Dk, (h + 1) * Dk)
                v = i_ref[rows, cols].astype(MXU_DT)
                sc = jnp.where(causal, _dot_nt(qe[:, cols], ktb[:, cols]), 0.0)
                o = _dot(sc.astype(MXU_DT), v) + _dot_nt(qe[:, cols], states[h].astype(MXU_DT))
                states[h] = states[h] * dec[:, cols] + _dot_tn(v, kdec[:, cols])
                finish(o, rows, cols)
        for h in range(HG_HEADS):
            state_ref[h] = states[h]

    lf = logf_s[...]
    total = jnp.min(jnp.concatenate(
        [jnp.sum(lf[c * C:(c + 1) * C], axis=0, keepdims=True) for c in range(ts // C)], axis=0))
    lax.cond(total >= HG_FAST_LIMIT, fast, robust)


def _hgrn(proj_h, loglb, log1mlb, onemlb, gw, amat, masks, B, S, ts=512):
    nt = S // ts
    sec = lambda s: (lambda b, t: (b * nt + t, s))
    vec = pl.BlockSpec((1, HG_WIDTH), lambda b, t: (0, 0))
    return pl.pallas_call(
        _hgrn_kernel,
        grid=(B, nt),
        in_specs=[pl.BlockSpec((ts, HG_WIDTH), sec(0)),
                  pl.BlockSpec((ts, HG_WIDTH), sec(1)),
                  pl.BlockSpec((ts, HG_WIDTH), sec(2)),
                  pl.BlockSpec((ts, HG_WIDTH), sec(3)),
                  vec, vec, vec,
                  pl.BlockSpec((1, HG_HEAD_DIM), lambda b, t: (0, 0)),
                  pl.BlockSpec(amat.shape, lambda b, t: (0, 0)),
                  pl.BlockSpec(masks.shape, lambda b, t: (0, 0, 0))],
        out_specs=pl.BlockSpec((ts, HG_WIDTH), lambda b, t: (b * nt + t, 0)),
        out_shape=jax.ShapeDtypeStruct((B * S, HG_WIDTH), MXU_DT),
        scratch_shapes=[pltpu.VMEM((HG_HEADS, HG_HEAD_DIM, HG_HEAD_DIM), F32),
                        pltpu.VMEM((ts, HG_WIDTH), F32),
                        pltpu.VMEM((ts, HG_WIDTH), F32)],
        compiler_params=_cparams(("arbitrary", "arbitrary")),
        name="hgrn2",
    )(proj_h, proj_h, proj_h, proj_h, loglb, log1mlb, onemlb, gw, amat, masks)


def _gelu_tanh(x):
    return 0.5 * x * (1.0 + jnp.tanh(math.sqrt(2.0 / math.pi) * (x + 0.044715 * (x * x * x))))


def _compress_kernel(kc_ref, vc_ref, pka_ref, pkb_ref, wka_ref, wkb_ref, wk2_ref,
                     pva_ref, pvb_ref, wva_ref, wvb_ref, wv2_ref, ok_ref, ov_ref):
    def one(x_ref, pa_ref, pb_ref, wa_ref, wb_ref, w2_ref):
        x = x_ref[...]
        n = x.shape[0]
        ha = _dot(x, wa_ref[...])
        hb = _dot(x, wb_ref[...])
        posw = _dot(pa_ref[...], wa_ref[...]) + _dot(pb_ref[...], wb_ref[...])
        h = ha + pltpu.roll(hb, n - 1, axis=0) + posw[0:1]
        return _dot(_gelu_tanh(h).astype(MXU_DT), w2_ref[...])

    ok_ref[...] = one(kc_ref, pka_ref, pkb_ref, wka_ref, wkb_ref, wk2_ref).astype(ok_ref.dtype)
    ov_ref[...] = one(vc_ref, pva_ref, pvb_ref, wva_ref, wvb_ref, wv2_ref).T.astype(ov_ref.dtype)


def _compress(kc3, vc3, kparams, vparams):
    B, n, w = kc3.shape
    x_spec = pl.BlockSpec((None, n, w), lambda b: (b, 0, 0))
    full = lambda a: pl.BlockSpec(a.shape, lambda b: (0,) * a.ndim)
    return pl.pallas_call(
        _compress_kernel,
        grid=(B,),
        in_specs=[x_spec, x_spec] + [full(a) for a in kparams] + [full(a) for a in vparams],
        out_specs=[pl.BlockSpec((None, n, KV_W), lambda b: (b, 0, 0)),
                   pl.BlockSpec((None, KV_W, n), lambda b: (b, 0, 0))],
        out_shape=[jax.ShapeDtypeStruct((B, n, KV_W), MXU_DT),
                   jax.ShapeDtypeStruct((B, KV_W, n), MXU_DT)],
        compiler_params=_cparams(("parallel",)),
        name="nsa_compress",
    )(kc3, vc3, *kparams, *vparams)


def _compress_params(pos, w1, w2):
    G, Dh, Hd = NSA_KV_GROUPS, NSA_HEAD_DIM, CMP_HIDDEN
    eye = jnp.eye(G, dtype=F32)

    def half(wh, ph):
        wbig = jnp.einsum('rdj,gk->rgdkj', wh.reshape(CMP_STRIDE, Dh, Hd), eye).reshape(CMP_STRIDE * G * Dh, G * Hd)
        pbig = jnp.broadcast_to(ph[:, None, :], (CMP_STRIDE, G, Dh)).reshape(1, CMP_STRIDE * G * Dh)
        return jnp.broadcast_to(pbig, (8, pbig.shape[1])).astype(MXU_DT), wbig.astype(MXU_DT)

    pa, wa = half(w1[:CMP_STRIDE * Dh], pos[:CMP_STRIDE])
    pb, wb = half(w1[CMP_STRIDE * Dh:], pos[CMP_STRIDE:])
    w2big = jnp.einsum('jd,gk->gjkd', w2, eye).reshape(G * Hd, G * Dh).astype(MXU_DT)
    return pa, pb, wa, wb, w2big


def _t5_bucket_np(n):
    n = np.maximum(n, 0)
    max_exact = N_BUCKETS // 2
    nf = np.maximum(n, 1).astype(np.float32)
    large = max_exact + (np.log(nf / np.float32(max_exact)) / np.float32(math.log(MAX_DISTANCE / max_exact))
                         * np.float32(N_BUCKETS - max_exact)).astype(np.int32)
    large = np.minimum(large, N_BUCKETS - 1)
    return np.where(n < max_exact, n, large).astype(np.int32)


def _nsa_tables(rel_bias):
    rb = rel_bias.astype(F32)
    heads = np.arange(NSA_HEADS)

    def table(dist, valid, rel_to_far):
        b = rb[jnp.asarray(_t5_bucket_np(dist))]
        if rel_to_far:
            b = b - rb[N_BUCKETS - 1]
        b = jnp.where(jnp.asarray(valid)[:, :, None], b, MASK_NEG)
        return jnp.transpose(b[:, :, heads], (0, 2, 1)).reshape(dist.shape[0], COLS)

    i = np.arange(TQ)[None, :]
    d_sel = i + TQ - np.arange(DIAG_W)[:, None]
    t_sel = table(d_sel, d_sel >= 0, True)
    d_win = i + WINDOW - np.arange(WIN_W)[:, None]
    t_win = table(d_win, (d_win >= 0) & (d_win < WINDOW), False)
    rel = np.arange(CMP_TAB)[:, None] - 16
    d_cmp = i - CMP_STRIDE * rel - (CMP_BLOCK - 1)
    t_cmp = table(d_cmp, d_cmp >= 0, True)
    return t_sel * LOG2E, t_win * LOG2E, t_cmp * LOG2E


def _nsa_consts(S):
    n_slc = S // SLC_BLOCK
    n_cmp_pad = S // CMP_STRIDE
    assert n_slc < LANES and n_cmp_pad >= CMP_BAND
    far = _t5_bucket_np(np.arange(8 * MAX_DISTANCE)) == N_BUCKETS - 1
    far_dist = int(np.max(np.nonzero(~far)[0])) + 1
    assert TQ + 1 >= far_dist and 17 * CMP_STRIDE - (CMP_BLOCK - 1) >= far_dist
    eind = np.zeros((S, LANES), np.float32)
    eind[np.arange(S), np.arange(S) // SLC_BLOCK] = 1.0
    cs = np.arange(n_cmp_pad)[None, :] * CMP_STRIDE
    ss = np.arange(LANES)[:, None] * SLC_BLOCK
    ov = np.clip(np.minimum(cs + CMP_BLOCK, ss + SLC_BLOCK) - np.maximum(cs, ss), 0, None)
    agg_t = (ov / CMP_STRIDE).astype(np.float32)
    agg_t[:, n_cmp_pad - 1] = 0.0
    agg_t[n_slc:] = 0.0
    return eind, agg_t


def _nsa_kernel(q_ref, gt_ref, kc_ref, vct_ref, ks_ref, vs_ref, kw_ref, vw_ref,
                tsel_ref, twin_ref, tcmp_ref, eind_ref, aggt_ref,
                o_ref, kaug_s, vselt_s, kwin_s, vwint_s, sc_s, m_s, l_s, acc_s):
    S = ks_ref.shape[0]
    n_slc = S // SLC_BLOCK
    n_cmp_pad = kc_ref.shape[0]
    qt = pl.program_id(1)
    t0 = pl.multiple_of(qt * TQ, TQ)

    @pl.when(qt == 0)
    def _():
        lane1 = lax.broadcasted_iota(jnp.int32, (1, LANES), 1)
        kaug_s[0:TQ, 0:LANES] = jnp.zeros((TQ, LANES), MXU_DT)
        kaug_s[0:TQ, LANES:] = jnp.broadcast_to(jnp.where(lane1 == LANES - 1, 1.0, 0.0).astype(MXU_DT), (TQ, LANES))
        kaug_s[TQ:, 0:LANES] = ks_ref[...]
        kaug_s[TQ:, LANES:] = eind_ref[...]
        vselt_s[:, 0:TQ] = jnp.zeros((LANES, TQ), MXU_DT)
        vselt_s[:, TQ:] = vs_ref[...].astype(F32).T.astype(MXU_DT)
        kwin_s[0:WINDOW, 0:LANES] = jnp.zeros((WINDOW, LANES), MXU_DT)
        kwin_s[0:WINDOW, LANES:] = jnp.broadcast_to(jnp.where(lane1 == 0, 1.0, 0.0).astype(MXU_DT), (WINDOW, LANES))
        kwin_s[WINDOW:, 0:LANES] = kw_ref[...]
        kwin_s[WINDOW:, LANES:] = jnp.zeros((S, LANES), MXU_DT)
        vwint_s[:, 0:WINDOW] = jnp.zeros((LANES, WINDOW), MXU_DT)
        vwint_s[:, WINDOW:] = vw_ref[...].astype(F32).T.astype(MXU_DT)

    lane = lax.broadcasted_iota(jnp.int32, (1, LANES), 1)
    lo_lane = lane < NSA_HEAD_DIM
    zero = jnp.zeros((), MXU_DT)

    qg = jnp.concatenate([jnp.where(lo_lane if g == 0 else jnp.logical_not(lo_lane),
                                    q_ref[:, j * LANES:(j + 1) * LANES], zero)
                          for g in range(NSA_KV_GROUPS) for j in range(NSA_HPG)], axis=0)

    sc_s[...] = _dot_nt(kc_ref[...], qg)
    n0 = pl.multiple_of(qt * (TQ // CMP_STRIDE), SUBLANES)
    band = pl.multiple_of(jnp.maximum(n0 - 16, 0), SUBLANES)
    toff = pl.multiple_of(16 + band - n0, SUBLANES)
    sc_s[pl.ds(band, CMP_BAND), :] += tcmp_ref[pl.ds(toff, CMP_BAND), :]
    nrow = lax.broadcasted_iota(jnp.int32, (n_cmp_pad, 1), 0)
    s_c = jnp.where(nrow < n0 + SUBLANES, sc_s[...], MASK_NEG)
    m_c = jnp.max(s_c, axis=0, keepdims=True)
    p_c = jnp.where(s_c > 0.5 * MASK_NEG, jnp.exp2(s_c - m_c), 0.0)
    l_c = jnp.sum(p_c, axis=0, keepdims=True)
    p_c = p_c * (1.0 / jnp.where(l_c > 0.0, l_c, 1.0))
    o_c = _dot(vct_ref[...], p_c.astype(MXU_DT))

    p_sum = jnp.concatenate(
        [sum(p_c[:, (g * NSA_HPG + j) * TQ:(g * NSA_HPG + j + 1) * TQ] for j in range(NSA_HPG))
         for g in range(NSA_KV_GROUPS)], axis=1)
    n_rows = -(-n_slc // SUBLANES) * SUBLANES
    imp = _dot_split(p_sum, aggt_ref[...], 3, x_is_rhs=True)[0:n_rows]
    blk = lax.broadcasted_iota(jnp.int32, (n_rows, 1), 0)
    blk_f = blk.astype(F32)
    tq_pos = t0 + lax.broadcasted_iota(jnp.int32, (1, TQ), 1)
    tpos = jnp.concatenate([tq_pos] * NSA_KV_GROUPS, axis=1)
    diff = tpos // SLC_BLOCK - blk
    forced = (blk == 0) | ((diff >= 0) & (diff < SLC_LOCAL))
    score = jnp.where(forced, BIG, jnp.where(blk * SLC_BLOCK <= tpos, imp, -BIG))
    score = jnp.where(blk < n_slc, score, -jnp.inf)
    sel = jnp.zeros(score.shape, F32)
    for _ in range(min(SLC_TOPK, n_slc)):
        mx = jnp.max(score, axis=0, keepdims=True)
        idx = jnp.min(jnp.where(score == mx, blk_f, float(LANES)), axis=0, keepdims=True)
        hit = blk_f == idx
        sel = jnp.where(hit, 1.0, sel)
        score = jnp.where(hit, -jnp.inf, score)
    if n_rows < LANES:
        sel = jnp.concatenate([sel, jnp.zeros((LANES - n_rows, sel.shape[1]), F32)], axis=0)
    selb = (sel - 1.0) * (-SEL_NEG)
    selb_t = [selb[:, g * TQ:(g + 1) * TQ].T.astype(MXU_DT) for g in range(NSA_KV_GROUPS)]
    qa = jnp.concatenate(
        [qg, jnp.concatenate([selb_t[g] for g in range(NSA_KV_GROUPS) for _ in range(NSA_HPG)], axis=0)],
        axis=1)

    m_s[...] = jnp.full(m_s.shape, MASK_NEG, F32)
    l_s[...] = jnp.zeros(l_s.shape, F32)
    acc_s[...] = jnp.zeros(acc_s.shape, F32)

    def update(s, vt):
        m_prev = m_s[...]
        m_new = jnp.maximum(m_prev, jnp.max(s, axis=0, keepdims=True))
        alpha = jnp.exp2(m_prev - m_new)
        p = jnp.exp2(s - m_new)
        l_s[...] = alpha * l_s[...] + jnp.sum(p, axis=0, keepdims=True)
        acc_s[...] = alpha * acc_s[...] + _dot(vt, p.astype(MXU_DT))
        m_s[...] = m_new

    def far_tile(width, sub):
        def body(i, start0):
            starts = [pl.multiple_of(TQ + start0 + i * width + u * sub, LANES) for u in range(width // sub)]
            scores = [_dot_nt(kaug_s[pl.ds(st, sub), :], qa) for st in starts]
            for st, s in zip(starts, scores):
                update(s, vselt_s[:, pl.ds(st, sub)])
            return start0
        return body

    n_far = jnp.maximum(t0 - TQ, 0)
    n_full = n_far // SEL_TK
    lax.fori_loop(0, n_full, far_tile(SEL_TK, SEL_SUB), 0)
    lax.fori_loop(0, (n_far - n_full * SEL_TK) // LANES, far_tile(LANES, LANES), n_full * SEL_TK)

    update(_dot_nt(kaug_s[pl.ds(t0, DIAG_W), :], qa) + tsel_ref[...], vselt_s[:, pl.ds(t0, DIAG_W)])
    o_s = acc_s[...] * (1.0 / l_s[...])

    flag = jnp.broadcast_to(jnp.where(lane == 0, SEL_NEG, 0.0).astype(MXU_DT), (COLS, LANES))
    s_w = _dot_nt(kwin_s[pl.ds(t0, WIN_W), :], jnp.concatenate([qg, flag], axis=1)) + twin_ref[...]
    p_w = jnp.exp2(s_w - jnp.max(s_w, axis=0, keepdims=True))
    o_w = _dot(vwint_s[:, pl.ds(t0, WIN_W)], p_w.astype(MXU_DT)) * (1.0 / jnp.sum(p_w, axis=0, keepdims=True))

    gsig_t = (1.0 / (1.0 + jnp.exp(-gt_ref[...]))).T
    row_lo = lax.broadcasted_iota(jnp.int32, (LANES, 1), 0) < NSA_HEAD_DIM
    for j in range(NSA_HPG):
        c0, c1 = j * TQ, (NSA_HPG + j) * TQ
        acc = None
        for br, o in enumerate((o_c, o_s, o_w)):
            r0, r1 = br * NSA_HEADS + j, br * NSA_HEADS + NSA_HPG + j
            gate = jnp.where(row_lo, gsig_t[r0:r0 + 1, :], gsig_t[r1:r1 + 1, :])
            term = gate * jnp.where(row_lo, o[:, c0:c0 + TQ], o[:, c1:c1 + TQ])
            acc = term if acc is None else acc + term
        o_ref[:, j * LANES:(j + 1) * LANES] = acc.T.astype(o_ref.dtype)


def _nsa(proj_n, gates, kcmp, vcmp_t, tsel, twin, tcmp, eind, agg_t, B, S):
    nq = S // TQ
    n_cmp_pad = S // CMP_STRIDE
    qcols = NSA_WIDTH // LANES
    tok = lambda b, t: (b * nq + t, 0)
    kv = lambda c: pl.BlockSpec((S, KV_W), lambda b, t: (b, qcols + c))
    const = lambda a: pl.BlockSpec(a.shape, lambda b, t: (0,) * a.ndim)
    return pl.pallas_call(
        _nsa_kernel,
        grid=(B, nq),
        in_specs=[pl.BlockSpec((TQ, NSA_WIDTH), tok),
                  pl.BlockSpec((TQ, LANES), tok),
                  pl.BlockSpec((None, n_cmp_pad, KV_W), lambda b, t: (b, 0, 0)),
                  pl.BlockSpec((None, KV_W, n_cmp_pad), lambda b, t: (b, 0, 0)),
                  kv(0), kv(1), kv(2), kv(3),
                  const(tsel), const(twin), const(tcmp), const(eind), const(agg_t)],
        out_specs=pl.BlockSpec((TQ, NSA_WIDTH), tok),
        out_shape=jax.ShapeDtypeStruct((B * S, NSA_WIDTH), MXU_DT),
        scratch_shapes=[pltpu.VMEM((TQ + S, 2 * LANES), MXU_DT),
                        pltpu.VMEM((LANES, TQ + S), MXU_DT),
                        pltpu.VMEM((WINDOW + S, 2 * LANES), MXU_DT),
                        pltpu.VMEM((LANES, WINDOW + S), MXU_DT),
                        pltpu.VMEM((n_cmp_pad, COLS), F32),
                        pltpu.VMEM((1, COLS), F32),
                        pltpu.VMEM((1, COLS), F32),
                        pltpu.VMEM((LANES, COLS), F32)],
        compiler_params=_cparams(("arbitrary", "arbitrary")),
        name="nsa_attention",
    )(proj_n, gates, kcmp, vcmp_t, proj_n, proj_n, proj_n, proj_n, tsel, twin, tcmp, eind, agg_t)


def _layer_norm(x, g, b):
    mu = jnp.mean(x, axis=-1, keepdims=True)
    xc = x - mu
    var = jnp.mean(xc * xc, axis=-1, keepdims=True)
    return xc * lax.rsqrt(var + 1e-5) * g + b


def _post_kernel(x_ref, oa_ref, ob_ref, woa_ref, wob_ref, g1_ref, b1_ref, wg_ref, wu_ref, wd_ref, g2_ref, b2_ref,
                 o_ref, acc_ref):
    mix = _dot(oa_ref[...], woa_ref[...]) + _dot(ob_ref[...], wob_ref[...])
    x1 = _layer_norm(ALPHA * x_ref[...] + mix, g1_ref[...], b1_ref[...])
    xb = x1.astype(MXU_DT)
    tf = 256
    for c in range(0, D_FF, tf):
        gate = _dot(xb, wg_ref[:, c:c + tf])
        up = _dot(xb, wu_ref[:, c:c + tf])
        h = (gate / (1.0 + jnp.exp(-gate)) * up).astype(MXU_DT)
        d = _dot(h, wd_ref[c:c + tf, :])
        if c == 0:
            acc_ref[...] = d
        else:
            acc_ref[...] += d
    o_ref[...] = _layer_norm(ALPHA * x1 + acc_ref[...], g2_ref[...], b2_ref[...])


def _post(x2, oa, ob, woa, wob, g1, b1, wg, wu, wd, g2, b2, tm=512):
    T = x2.shape[0]
    row = lambda i: (i, 0)
    const = lambda a: pl.BlockSpec(a.shape, lambda i: (0, 0), pipeline_mode=pl.Buffered(1))
    return pl.pallas_call(
        _post_kernel,
        grid=(T // tm,),
        in_specs=[pl.BlockSpec((tm, D_MODEL), row),
                  pl.BlockSpec((tm, HG_WIDTH), row),
                  pl.BlockSpec((tm, NSA_WIDTH), row),
                  const(woa), const(wob), const(g1), const(b1),
                  const(wg), const(wu), const(wd), const(g2), const(b2)],
        out_specs=pl.BlockSpec((tm, D_MODEL), row),
        out_shape=jax.ShapeDtypeStruct((T, D_MODEL), F32),
        scratch_shapes=[pltpu.VMEM((tm, D_MODEL), F32)],
        compiler_params=_cparams(("parallel",)),
        name="outproj_ffn",
    )(x2, oa, ob, woa, wob, g1, b1, wg, wu, wd, g2, b2)


def _split_w_in(w):
    o = 4 * HG_WIDTH
    wh = w[:, :o]
    wq = w[:, o:o + NSA_WIDTH].reshape(D_MODEL, NSA_HEADS, NSA_HEAD_DIM)[:, list(HEAD_PERM)].reshape(D_MODEL, NSA_WIDTH)
    wq = wq * (NSA_HEAD_DIM ** -0.5 * LOG2E)
    o += NSA_WIDTH
    kc, vc, ks, vs, kw, vw = (w[:, o + i * KV_W:o + (i + 1) * KV_W] for i in range(6))
    o += 6 * KV_W
    wgt = jnp.pad(w[:, o:], ((0, 0), (0, LANES - 3 * NSA_HEADS)))
    wn = jnp.concatenate([wq, ks, vs, kw, vw], axis=1)
    wc = jnp.concatenate([kc, vc], axis=1)
    return wh.astype(MXU_DT), wn.astype(MXU_DT), wc.astype(MXU_DT), wgt.astype(MXU_DT)


def kernel(x, w_in, hg_lb_param, hg_norm_w, cmp_pos_k, cmp_w1_k, cmp_w2_k, cmp_pos_v, cmp_w1_v, cmp_w2_v,
           rel_bias, w_out, ln1_g, ln1_b, w_ffn_gate, w_ffn_up, w_ffn_down, ln2_g, ln2_b):
    B, S, _ = x.shape
    T = B * S
    depth = w_in.shape[0]

    p = jax.nn.softmax(hg_lb_param.astype(F32), axis=0)
    c = jnp.cumsum(p, axis=0)
    lbs = c - c[0:1]
    loglb, log1mlb, onemlb = jnp.log(lbs), jnp.log1p(-lbs), 1.0 - lbs

    amat_np, masks_np = _hgrn_consts()
    amat, masks = jnp.asarray(amat_np, MXU_DT), jnp.asarray(masks_np)
    eind_np, aggt_np = _nsa_consts(S)
    eind, agg_t = jnp.asarray(eind_np, MXU_DT), jnp.asarray(aggt_np, MXU_DT)
    tsel, twin, tcmp = _nsa_tables(rel_bias)
    out_perm = np.concatenate([np.arange(h * NSA_HEAD_DIM, (h + 1) * NSA_HEAD_DIM) for h in HEAD_PERM])

    x2 = x.reshape(T, D_MODEL).astype(F32)
    for l in range(depth):
        wh, wn, wc, wgt = _split_w_in(w_in[l])
        proj_h, proj_n, kc, vc, gates = _inproj(x2, wh, wn, wc, wgt)
        o_a = _hgrn(proj_h, loglb[l:l + 1], log1mlb[l:l + 1], onemlb[l:l + 1], hg_norm_w[l][None, :], amat, masks, B, S)
        shape3 = (B, S // CMP_STRIDE, CMP_STRIDE * KV_W)
        kcmp, vcmp_t = _compress(kc.reshape(shape3), vc.reshape(shape3),
                                 _compress_params(cmp_pos_k[l], cmp_w1_k[l], cmp_w2_k[l]),
                                 _compress_params(cmp_pos_v[l], cmp_w1_v[l], cmp_w2_v[l]))
        o_b = _nsa(proj_n, gates, kcmp, vcmp_t, tsel, twin, tcmp, eind, agg_t, B, S)
        woa = w_out[l][:HG_WIDTH].astype(MXU_DT)
        wob = w_out[l][HG_WIDTH:][out_perm].astype(MXU_DT)
        x2 = _post(x2, o_a, o_b, woa, wob, ln1_g[l][None, :], ln1_b[l][None, :],
                   w_ffn_gate[l].astype(MXU_DT), w_ffn_up[l].astype(MXU_DT), w_ffn_down[l].astype(MXU_DT),
                   ln2_g[l][None, :], ln2_b[l][None, :])
    return x2.reshape(B, S, D_MODEL).astype(x.dtype)
```

```python
import math

import numpy as np
import jax
import jax.numpy as jnp
from jax import lax
from jax.experimental import pallas as pl
from jax.experimental.pallas import tpu as pltpu

F32 = jnp.float32
MXU_DT = jnp.bfloat16

D_MODEL = 1024
DEPTH = 4
HG_WIDTH = 512
HG_HEAD_DIM = 128
HG_HEADS = 4
HG_CHUNK = 64
NSA_WIDTH = 512
NSA_HEAD_DIM = 64
NSA_HEADS = 8
NSA_KV_GROUPS = 2
NSA_HPG = 4
CMP_BLOCK = 32
CMP_STRIDE = 16
CMP_HIDDEN = 128
SLC_BLOCK = 64
SLC_TOPK = 16
SLC_LOCAL = 2
WINDOW = 512
N_BUCKETS = 32
MAX_DISTANCE = 128
D_FF = 2816
ALPHA = (2.0 * DEPTH) ** 0.25
KV_W = NSA_KV_GROUPS * NSA_HEAD_DIM
LANES = 128
SUBLANES = 8
BIG = 1e9
MASK_NEG = -1e30
SEL_NEG = MASK_NEG
LOG2E = math.log2(math.e)

VMEM_LIMIT = 56 * 1024 * 1024

TQ = 128
COLS = NSA_HEADS * TQ
SEL_TK = 512
BOUND_LIMIT = 40.0
BOUND_SLACK = 1.01
VROWS = LANES + 16
DIAG_W = 2 * TQ
WIN_W = WINDOW + TQ
CMP_BAND = 24
CMP_TAB = 40
HEAD_PERM = tuple(g * NSA_HPG + j for j in range(NSA_HPG) for g in range(NSA_KV_GROUPS))


def _dot(a, b):
    return jnp.dot(a, b, preferred_element_type=F32)


def _dot_nt(a, b):
    return lax.dot_general(a, b, (((1,), (1,)), ((), ())), preferred_element_type=F32)


def _dot_tn(a, b):
    return lax.dot_general(a, b, (((0,), (0,)), ((), ())), preferred_element_type=F32)


def _dot_split(x, w, parts, x_is_rhs=False):
    acc = None
    r = x
    for p in range(parts):
        h = r.astype(MXU_DT)
        d = _dot(w, h) if x_is_rhs else _dot(h, w)
        acc = d if acc is None else acc + d
        if p + 1 < parts:
            r = r - h.astype(F32)
    return acc


def _cparams(sem):
    return pltpu.CompilerParams(dimension_semantics=sem, vmem_limit_bytes=VMEM_LIMIT)


def _inproj_kernel(x_ref, wh_ref, wn_ref, wc_ref, wg_ref, oh_ref, on_ref, okc_ref, ovc_ref, og_ref):
    xb = x_ref[...].astype(MXU_DT)
    nh = oh_ref.shape[1]
    for c in range(0, nh, 512):
        oh_ref[:, c:c + 512] = _dot(xb, wh_ref[:, c:c + 512])
    nn = on_ref.shape[1]
    for c in range(0, nn, 512):
        w = min(512, nn - c)
        on_ref[:, c:c + w] = _dot(xb, wn_ref[:, c:c + w]).astype(on_ref.dtype)
    kv = _dot(xb, wc_ref[...])
    okc_ref[...] = kv[:, :KV_W].astype(okc_ref.dtype)
    ovc_ref[...] = kv[:, KV_W:].astype(ovc_ref.dtype)
    og_ref[...] = _dot(xb, wg_ref[...])


def _inproj(x2, wh, wn, wc, wg, tm=512):
    T = x2.shape[0]
    nh, nn = wh.shape[1], wn.shape[1]
    full = lambda i: (0, 0)
    row = lambda i: (i, 0)
    return pl.pallas_call(
        _inproj_kernel,
        grid=(T // tm,),
        in_specs=[pl.BlockSpec((tm, D_MODEL), row),
                  pl.BlockSpec((D_MODEL, nh), full),
                  pl.BlockSpec((D_MODEL, nn), full),
                  pl.BlockSpec((D_MODEL, 2 * KV_W), full),
                  pl.BlockSpec((D_MODEL, LANES), full)],
        out_specs=[pl.BlockSpec((tm, nh), row),
                   pl.BlockSpec((tm, nn), row),
                   pl.BlockSpec((tm, KV_W), row),
                   pl.BlockSpec((tm, KV_W), row),
                   pl.BlockSpec((tm, LANES), row)],
        out_shape=[jax.ShapeDtypeStruct((T, nh), F32),
                   jax.ShapeDtypeStruct((T, nn), MXU_DT),
                   jax.ShapeDtypeStruct((T, KV_W), MXU_DT),
                   jax.ShapeDtypeStruct((T, KV_W), MXU_DT),
                   jax.ShapeDtypeStruct((T, LANES), F32)],
        compiler_params=_cparams(("parallel",)),
        name="inproj",
    )(x2, wh, wn, wc, wg)


_HG_LEVELS = (32, 16, 8, 4, 2, 1)
_HG_NLEV = len(_HG_LEVELS)
HG_FAST_LIMIT = -80.0
HG_PREP_ROWS = 16


def _hgrn_consts():
    C = HG_CHUNK
    a = np.zeros((8 * C, C), np.float32)
    m = np.zeros((_HG_NLEV + 1, C, C), np.float32)
    for l, h in enumerate(_HG_LEVELS):
        for r in range(C):
            mid = (r // (2 * h)) * 2 * h + h
            if r >= mid:
                a[l * C + r, mid:r + 1] = 1.0
                m[l, r, mid - h:mid] = 1.0
            else:
                a[l * C + r, r + 1:mid] = -1.0
    for r in range(C):
        a[6 * C + r, :r + 1] = 1.0
        a[7 * C + r, r + 1:] = 1.0
        m[_HG_NLEV, r, r] = 1.0
    return a, m


def _hgrn_kernel(q_ref, f_ref, i_ref, g_ref, loglb_ref, log1mlb_ref, onemlb_ref, gw_ref, amat_ref, mask_ref,
                 o_ref, state_ref, logf_s, k_s):
    C, Dk = HG_CHUNK, HG_HEAD_DIM
    ts = q_ref.shape[0]

    @pl.when(pl.program_id(1) == 0)
    def _():
        state_ref[...] = jnp.zeros_like(state_ref)

    a = loglb_ref[...]
    for r in range(0, ts, HG_PREP_ROWS):
        rows = slice(r, r + HG_PREP_ROWS)
        z = f_ref[rows, :]
        e = jnp.exp(-jnp.abs(z))
        e1 = 1.0 + e
        bt = log1mlb_ref[...] + (jnp.minimum(z, 0.0) - jnp.log(e1))
        logf_s[rows, :] = jnp.maximum(a, bt) + jnp.log(1.0 + jnp.exp(-jnp.abs(a - bt)))
        k_s[rows, :] = onemlb_ref[...] * jnp.where(z > 0.0, e, 1.0) / e1

    def finish(o, rows, cols):
        o = o * lax.rsqrt(jnp.mean(o * o, axis=-1, keepdims=True) + 1e-6) * gw_ref[...]
        g = g_ref[rows, cols]
        o_ref[rows, cols] = (o * (g / (1.0 + jnp.exp(-g)))).astype(o_ref.dtype)

    def robust():
        amat = amat_ref[...]

        def chunk(c, carry):
            r0 = pl.multiple_of(c * C, C)
            rows = pl.ds(r0, C)
            for h in range(HG_HEADS):
                cols = slice(h * Dk, (h + 1) * Dk)
                d = _dot_split(logf_s[rows, cols], amat, 2, x_is_rhs=True)
                q = q_ref[rows, cols]
                k = k_s[rows, cols]
                v = i_ref[rows, cols].astype(MXU_DT)
                sc = _dot_nt(q.astype(MXU_DT), k.astype(MXU_DT)) * mask_ref[_HG_NLEV]
                for l in range(_HG_NLEV):
                    el = jnp.exp(-jnp.abs(d[l * C:(l + 1) * C]))
                    sc = sc + _dot_nt((q * el).astype(MXU_DT), (k * el).astype(MXU_DT)) * mask_ref[l]
                b = d[6 * C:7 * C]
                st = state_ref[h]
                o = _dot(sc.astype(MXU_DT), v) + _dot_nt((q * jnp.exp(b)).astype(MXU_DT), st.astype(MXU_DT))
                kdec = (k * jnp.exp(d[7 * C:8 * C])).astype(MXU_DT)
                state_ref[h] = st * jnp.exp(b[C - 1:C]) + _dot_tn(v, kdec)
                finish(o, rows, cols)
            return carry

        lax.fori_loop(0, ts // C, chunk, 0)

    def fast():
        tril_mat = amat_ref[6 * C:7 * C, :]
        causal = lax.broadcasted_iota(jnp.int32, (C, C), 0) >= lax.broadcasted_iota(jnp.int32, (C, C), 1)
        states = [state_ref[h] for h in range(HG_HEADS)]
        for c in range(ts // C):
            rows = slice(c * C, (c + 1) * C)
            b = _dot_split(logf_s[rows, :], tril_mat, 2, x_is_rhs=True)
            e = jnp.exp(b)
            dec = e[C - 1:C]
            qe = (q_ref[rows, :] * e).astype(MXU_DT)
            kt = k_s[rows, :] * jnp.exp(-b)
            ktb = kt.astype(MXU_DT)
            kdec = (kt * dec).astype(MXU_DT)
            for h in range(HG_HEADS):
                cols = slice(h * Dk, (h + 1) * Dk)
                v = i_ref[rows, cols].astype(MXU_DT)
                sc = jnp.where(causal, _dot_nt(qe[:, cols], ktb[:, cols]), 0.0)
                o = _dot(sc.astype(MXU_DT), v) + _dot_nt(qe[:, cols], states[h].astype(MXU_DT))
                states[h] = states[h] * dec[:, cols] + _dot_tn(v, kdec[:, cols])
                finish(o, rows, cols)
        for h in range(HG_HEADS):
            state_ref[h] = states[h]

    lf = logf_s[...]
    total = jnp.min(jnp.concatenate(
        [jnp.sum(lf[c * C:(c + 1) * C], axis=0, keepdims=True) for c in range(ts // C)], axis=0))
    lax.cond(total >= HG_FAST_LIMIT, fast, robust)


def _hgrn(proj_h, loglb, log1mlb, onemlb, gw, amat, masks, B, S, ts=512):
    nt = S // ts
    sec = lambda s: (lambda b, t: (b * nt + t, s))
    vec = pl.BlockSpec((1, HG_WIDTH), lambda b, t: (0, 0))
    return pl.pallas_call(
        _hgrn_kernel,
        grid=(B, nt),
        in_specs=[pl.BlockSpec((ts, HG_WIDTH), sec(0)),
                  pl.BlockSpec((ts, HG_WIDTH), sec(1)),
                  pl.BlockSpec((ts, HG_WIDTH), sec(2)),
                  pl.BlockSpec((ts, HG_WIDTH), sec(3)),
                  vec, vec, vec,
                  pl.BlockSpec((1, HG_HEAD_DIM), lambda b, t: (0, 0)),
                  pl.BlockSpec(amat.shape, lambda b, t: (0, 0)),
                  pl.BlockSpec(masks.shape, lambda b, t: (0, 0, 0))],
        out_specs=pl.BlockSpec((ts, HG_WIDTH), lambda b, t: (b * nt + t, 0)),
        out_shape=jax.ShapeDtypeStruct((B * S, HG_WIDTH), MXU_DT),
        scratch_shapes=[pltpu.VMEM((HG_HEADS, HG_HEAD_DIM, HG_HEAD_DIM), F32),
                        pltpu.VMEM((ts, HG_WIDTH), F32),
                        pltpu.VMEM((ts, HG_WIDTH), F32)],
        compiler_params=_cparams(("arbitrary", "arbitrary")),
        name="hgrn2",
    )(proj_h, proj_h, proj_h, proj_h, loglb, log1mlb, onemlb, gw, amat, masks)


def _gelu_tanh(x):
    return 0.5 * x * (1.0 + jnp.tanh(math.sqrt(2.0 / math.pi) * (x + 0.044715 * (x * x * x))))


def _compress_kernel(kc_ref, vc_ref, pka_ref, pkb_ref, wka_ref, wkb_ref, wk2_ref,
                     pva_ref, pvb_ref, wva_ref, wvb_ref, wv2_ref, ok_ref, ov_ref):
    def one(x_ref, pa_ref, pb_ref, wa_ref, wb_ref, w2_ref):
        x = x_ref[...]
        n = x.shape[0]
        ha = _dot(x, wa_ref[...])
        hb = _dot(x, wb_ref[...])
        posw = _dot(pa_ref[...], wa_ref[...]) + _dot(pb_ref[...], wb_ref[...])
        h = ha + pltpu.roll(hb, n - 1, axis=0) + posw[0:1]
        return _dot(_gelu_tanh(h).astype(MXU_DT), w2_ref[...])

    ok_ref[...] = one(kc_ref, pka_ref, pkb_ref, wka_ref, wkb_ref, wk2_ref).astype(ok_ref.dtype)
    ov_ref[...] = one(vc_ref, pva_ref, pvb_ref, wva_ref, wvb_ref, wv2_ref).T.astype(ov_ref.dtype)


def _compress(kc3, vc3, kparams, vparams):
    B, n, w = kc3.shape
    x_spec = pl.BlockSpec((None, n, w), lambda b: (b, 0, 0))
    full = lambda a: pl.BlockSpec(a.shape, lambda b: (0,) * a.ndim)
    return pl.pallas_call(
        _compress_kernel,
        grid=(B,),
        in_specs=[x_spec, x_spec] + [full(a) for a in kparams] + [full(a) for a in vparams],
        out_specs=[pl.BlockSpec((None, n, KV_W), lambda b: (b, 0, 0)),
                   pl.BlockSpec((None, KV_W, n), lambda b: (b, 0, 0))],
        out_shape=[jax.ShapeDtypeStruct((B, n, KV_W), MXU_DT),
                   jax.ShapeDtypeStruct((B, KV_W, n), MXU_DT)],
        compiler_params=_cparams(("parallel",)),
        name="nsa_compress",
    )(kc3, vc3, *kparams, *vparams)


def _compress_params(pos, w1, w2):
    G, Dh, Hd = NSA_KV_GROUPS, NSA_HEAD_DIM, CMP_HIDDEN
    eye = jnp.eye(G, dtype=F32)

    def half(wh, ph):
        wbig = jnp.einsum('rdj,gk->rgdkj', wh.reshape(CMP_STRIDE, Dh, Hd), eye).reshape(CMP_STRIDE * G * Dh, G * Hd)
        pbig = jnp.broadcast_to(ph[:, None, :], (CMP_STRIDE, G, Dh)).reshape(1, CMP_STRIDE * G * Dh)
        return jnp.broadcast_to(pbig, (8, pbig.shape[1])).astype(MXU_DT), wbig.astype(MXU_DT)

    pa, wa = half(w1[:CMP_STRIDE * Dh], pos[:CMP_STRIDE])
    pb, wb = half(w1[CMP_STRIDE * Dh:], pos[CMP_STRIDE:])
    w2big = jnp.einsum('jd,gk->gjkd', w2, eye).reshape(G * Hd, G * Dh).astype(MXU_DT)
    return pa, pb, wa, wb, w2big


def _t5_bucket_np(n):
    n = np.maximum(n, 0)
    max_exact = N_BUCKETS // 2
    nf = np.maximum(n, 1).astype(np.float32)
    large = max_exact + (np.log(nf / np.float32(max_exact)) / np.float32(math.log(MAX_DISTANCE / max_exact))
                         * np.float32(N_BUCKETS - max_exact)).astype(np.int32)
    large = np.minimum(large, N_BUCKETS - 1)
    return np.where(n < max_exact, n, large).astype(np.int32)


def _nsa_tables(rel_bias):
    rb = rel_bias.astype(F32)
    heads = np.arange(NSA_HEADS)

    def table(dist, valid, rel_to_far):
        b = rb[jnp.asarray(_t5_bucket_np(dist))]
        if rel_to_far:
            b = b - rb[N_BUCKETS - 1]
        b = jnp.where(jnp.asarray(valid)[:, :, None], b, MASK_NEG)
        return jnp.transpose(b[:, :, heads], (0, 2, 1)).reshape(dist.shape[0], COLS)

    i = np.arange(TQ)[None, :]
    d_sel = i + TQ - np.arange(DIAG_W)[:, None]
    t_sel = table(d_sel, d_sel >= 0, True)
    d_win = i + WINDOW - np.arange(WIN_W)[:, None]
    t_win = table(d_win, (d_win >= 0) & (d_win < WINDOW), False)
    rel = np.arange(CMP_TAB)[:, None] - 16
    d_cmp = i - CMP_STRIDE * rel - (CMP_BLOCK - 1)
    t_cmp = table(d_cmp, d_cmp >= 0, True)
    return t_sel * LOG2E, t_win * LOG2E, t_cmp * LOG2E


def _nsa_consts(S):
    n_slc = S // SLC_BLOCK
    n_cmp_pad = S // CMP_STRIDE
    assert n_slc < LANES and n_cmp_pad >= CMP_BAND
    far = _t5_bucket_np(np.arange(8 * MAX_DISTANCE)) == N_BUCKETS - 1
    far_dist = int(np.max(np.nonzero(~far)[0])) + 1
    assert TQ + 1 >= far_dist and 17 * CMP_STRIDE - (CMP_BLOCK - 1) >= far_dist
    eind = np.zeros((S, LANES), np.float32)
    eind[np.arange(S), np.arange(S) // SLC_BLOCK] = 1.0
    cs = np.arange(n_cmp_pad)[None, :] * CMP_STRIDE
    ss = np.arange(LANES)[:, None] * SLC_BLOCK
    ov = np.clip(np.minimum(cs + CMP_BLOCK, ss + SLC_BLOCK) - np.maximum(cs, ss), 0, None)
    agg_t = (ov / CMP_STRIDE).astype(np.float32)
    agg_t[:, n_cmp_pad - 1] = 0.0
    agg_t[n_slc:] = 0.0
    assert n_cmp_pad <= SUBLANES * LANES
    cind = np.zeros((n_cmp_pad, LANES), np.float32)
    cind[np.arange(n_cmp_pad), np.arange(n_cmp_pad) // SUBLANES] = 1.0
    return eind, agg_t, cind


def _nsa_kernel(q_ref, gt_ref, kc_ref, vct_ref, ks_ref, vs_ref, kw_ref, vw_ref,
                tsel_ref, twin_ref, tcmp_ref, eind_ref, aggt_ref, cind_ref, tmax_ref,
                o_ref, kaug_s, vselt_s, kwin_s, vwint_s, kcaug_s, sc_s, m_s, acc_s, osel_s, owin_s, knorm_s):
    S = ks_ref.shape[0]
    n_slc = S // SLC_BLOCK
    n_cmp_pad = kc_ref.shape[0]
    qt = pl.program_id(1)
    t0 = pl.multiple_of(qt * TQ, TQ)

    @pl.when(qt == 0)
    def _():
        lane1 = lax.broadcasted_iota(jnp.int32, (1, LANES), 1)
        kaug_s[0:TQ, 0:LANES] = jnp.zeros((TQ, LANES), MXU_DT)
        kaug_s[0:TQ, LANES:] = jnp.broadcast_to(jnp.where(lane1 == LANES - 1, 1.0, 0.0).astype(MXU_DT), (TQ, LANES))
        kaug_s[TQ:, 0:LANES] = ks_ref[...]
        kaug_s[TQ:, LANES:] = eind_ref[...]
        kcaug_s[:, 0:LANES] = kc_ref[...]
        kcaug_s[:, LANES:] = cind_ref[...]
        rg = lax.broadcasted_iota(jnp.int32, (LANES, LANES), 0) // NSA_HEAD_DIM
        cg = lax.broadcasted_iota(jnp.int32, (LANES, LANES), 1) // NSA_HEAD_DIM
        same_group = jnp.where(rg == cg, 1.0, 0.0).astype(MXU_DT)
        for i, ref in enumerate((ks_ref, kw_ref)):
            n2 = _dot(jnp.square(ref[...].astype(F32)).astype(MXU_DT), same_group)
            n2 = jnp.max(n2, axis=0, keepdims=True)
            for g in range(NSA_KV_GROUPS):
                knorm_s[i * NSA_KV_GROUPS + g] = jnp.sqrt(jnp.max(n2[:, g * NSA_HEAD_DIM:(g + 1) * NSA_HEAD_DIM]))
        ones_rows = (lax.broadcasted_iota(jnp.int32, (VROWS - LANES, 1), 0) == 0).astype(MXU_DT)
        vselt_s[0:LANES, 0:TQ] = jnp.zeros((LANES, TQ), MXU_DT)
        vselt_s[0:LANES, TQ:] = vs_ref[...].astype(F32).T.astype(MXU_DT)
        vselt_s[LANES:, :] = jnp.broadcast_to(ones_rows, (VROWS - LANES, TQ + S))
        kwin_s[0:WINDOW, 0:LANES] = jnp.zeros((WINDOW, LANES), MXU_DT)
        kwin_s[0:WINDOW, LANES:] = jnp.broadcast_to(jnp.where(lane1 == 0, 1.0, 0.0).astype(MXU_DT), (WINDOW, LANES))
        kwin_s[WINDOW:, 0:LANES] = kw_ref[...]
        kwin_s[WINDOW:, LANES:] = jnp.zeros((S, LANES), MXU_DT)
        vwint_s[0:LANES, 0:WINDOW] = jnp.zeros((LANES, WINDOW), MXU_DT)
        vwint_s[0:LANES, WINDOW:] = vw_ref[...].astype(F32).T.astype(MXU_DT)
        vwint_s[LANES:, :] = jnp.broadcast_to(ones_rows, (VROWS - LANES, WINDOW + S))

    lane = lax.broadcasted_iota(jnp.int32, (1, LANES), 1)
    lo_lane = lane < NSA_HEAD_DIM
    zero = jnp.zeros((), MXU_DT)

    qg = jnp.concatenate([jnp.where(lo_lane if g == 0 else jnp.logical_not(lo_lane),
                                    q_ref[:, j * LANES:(j + 1) * LANES], zero)
                          for g in range(NSA_KV_GROUPS) for j in range(NSA_HPG)], axis=0)

    n0 = pl.multiple_of(qt * (TQ // CMP_STRIDE), SUBLANES)
    late = jnp.where(lane > n0 // SUBLANES, MASK_NEG, 0.0).astype(MXU_DT)
    sc_s[...] = _dot_nt(kcaug_s[...], jnp.concatenate([qg, jnp.broadcast_to(late, (COLS, LANES))], axis=1))
    band = pl.multiple_of(jnp.maximum(n0 - 16, 0), SUBLANES)
    toff = pl.multiple_of(16 + band - n0, SUBLANES)
    sc_s[pl.ds(band, CMP_BAND), :] += tcmp_ref[pl.ds(toff, CMP_BAND), :]
    s_c = sc_s[...]
    m_c = jnp.max(s_c, axis=0, keepdims=True)
    p_c = jnp.exp2(s_c - m_c)
    l_c = jnp.sum(p_c, axis=0, keepdims=True)
    p_c = p_c * jnp.where(m_c > 0.5 * MASK_NEG, 1.0 / l_c, 0.0)
    o_c = _dot(vct_ref[...], p_c.astype(MXU_DT))

    p_sum = jnp.concatenate(
        [sum(p_c[:, (g * NSA_HPG + j) * TQ:(g * NSA_HPG + j + 1) * TQ] for j in range(NSA_HPG))
         for g in range(NSA_KV_GROUPS)], axis=1)
    n_rows = -(-n_slc // SUBLANES) * SUBLANES
    imp = _dot_split(p_sum, aggt_ref[...], 3, x_is_rhs=True)[0:n_rows]
    blk = lax.broadcasted_iota(jnp.int32, (n_rows, 1), 0)
    blk_f = blk.astype(F32)
    tq_pos = t0 + lax.broadcasted_iota(jnp.int32, (1, TQ), 1)
    tpos = jnp.concatenate([tq_pos] * NSA_KV_GROUPS, axis=1)
    diff = tpos // SLC_BLOCK - blk
    forced = (blk == 0) | ((diff >= 0) & (diff < SLC_LOCAL))
    score = jnp.where(forced, BIG, jnp.where(blk * SLC_BLOCK <= tpos, imp, -BIG))
    score = jnp.where(blk < n_slc, score, -jnp.inf)
    sel = jnp.zeros(score.shape, F32)
    for _ in range(min(SLC_TOPK, n_slc)):
        mx = jnp.max(score, axis=0, keepdims=True)
        idx = jnp.min(jnp.where(score == mx, blk_f, float(LANES)), axis=0, keepdims=True)
        hit = blk_f == idx
        sel = jnp.where(hit, 1.0, sel)
        score = jnp.where(hit, -jnp.inf, score)
    if n_rows < LANES:
        sel = jnp.concatenate([sel, jnp.zeros((LANES - n_rows, sel.shape[1]), F32)], axis=0)
    selb = (sel - 1.0) * (-SEL_NEG)
    selb_t = [selb[:, g * TQ:(g + 1) * TQ].T for g in range(NSA_KV_GROUPS)]
    selb_rows = jnp.concatenate([selb_t[g] for g in range(NSA_KV_GROUPS) for _ in range(NSA_HPG)], axis=0)

    qn = jnp.sqrt(_dot_nt(jnp.ones((SUBLANES, LANES), MXU_DT), jnp.square(qg.astype(F32)).astype(MXU_DT))[0:1])
    first_group = lax.broadcasted_iota(jnp.int32, (1, COLS), 1) < COLS // NSA_KV_GROUPS
    b_sel = qn * jnp.where(first_group, knorm_s[0], knorm_s[1]) * BOUND_SLACK + (tmax_ref[0] + 1.0)
    b_win = qn * jnp.where(first_group, knorm_s[2], knorm_s[3]) * BOUND_SLACK + (tmax_ref[1] + 1.0)
    bounded = jnp.maximum(jnp.max(b_sel), jnp.max(b_win)) <= BOUND_LIMIT
    qa = jnp.concatenate([qg, selb_rows.astype(MXU_DT)], axis=1)
    qw = jnp.concatenate(
        [qg, jnp.broadcast_to(jnp.where(lane == 0, SEL_NEG, 0.0).astype(MXU_DT), (COLS, LANES))], axis=1)

    def attend(use_bound):
        if not use_bound:
            m_s[...] = jnp.full(m_s.shape, MASK_NEG, F32)
        acc_s[...] = jnp.zeros(acc_s.shape, F32)

        def update(start, width, table_ref=None):
            s = _dot_nt(kaug_s[pl.ds(start, width), :], qa)
            if table_ref is not None:
                s = s + table_ref[...]
            vt = vselt_s[:, pl.ds(start, width)]
            if use_bound:
                acc_s[...] += _dot(vt, jnp.exp2((s - b_sel).astype(MXU_DT)))
            else:
                m_prev = m_s[...]
                m_new = jnp.maximum(m_prev, jnp.max(s, axis=0, keepdims=True))
                p = jnp.exp2((s - m_new).astype(MXU_DT))
                acc_s[...] = jnp.exp2(m_prev - m_new) * acc_s[...] + _dot(vt, p)
                m_s[...] = m_new

        def far_tile(width):
            def body(i, start0):
                update(pl.multiple_of(TQ + start0 + i * width, LANES), width)
                return start0
            return body

        n_far = jnp.maximum(t0 - TQ, 0)
        n_full = n_far // SEL_TK
        lax.fori_loop(0, n_full, far_tile(SEL_TK), 0)
        lax.fori_loop(0, (n_far - n_full * SEL_TK) // LANES, far_tile(LANES), n_full * SEL_TK)
        update(t0, DIAG_W, tsel_ref)
        acc = acc_s[...]
        osel_s[...] = acc[0:LANES] * (1.0 / acc[LANES:LANES + 1])

        s_w = _dot_nt(kwin_s[pl.ds(t0, WIN_W), :], qw) + twin_ref[...]
        s_w = s_w - (b_win if use_bound else jnp.max(s_w, axis=0, keepdims=True))
        acc = _dot(vwint_s[:, pl.ds(t0, WIN_W)], jnp.exp2(s_w.astype(MXU_DT)))
        owin_s[...] = acc[0:LANES] * (1.0 / acc[LANES:LANES + 1])

    lax.cond(bounded, lambda: attend(True), lambda: attend(False))
    o_s = osel_s[...]
    o_w = owin_s[...]

    gsig_t = (1.0 / (1.0 + jnp.exp(-gt_ref[...]))).T
    row_lo = lax.broadcasted_iota(jnp.int32, (LANES, 1), 0) < NSA_HEAD_DIM
    for j in range(NSA_HPG):
        c0, c1 = j * TQ, (NSA_HPG + j) * TQ
        acc = None
        for br, o in enumerate((o_c, o_s, o_w)):
            r0, r1 = br * NSA_HEADS + j, br * NSA_HEADS + NSA_HPG + j
            gate = jnp.where(row_lo, gsig_t[r0:r0 + 1, :], gsig_t[r1:r1 + 1, :])
            term = gate * jnp.where(row_lo, o[:, c0:c0 + TQ], o[:, c1:c1 + TQ])
            acc = term if acc is None else acc + term
        o_ref[:, j * LANES:(j + 1) * LANES] = acc.T.astype(o_ref.dtype)


def _nsa(proj_n, gates, kcmp, vcmp_t, tsel, twin, tcmp, eind, agg_t, cind, B, S):
    nq = S // TQ
    n_cmp_pad = S // CMP_STRIDE
    qcols = NSA_WIDTH // LANES
    tok = lambda b, t: (b * nq + t, 0)
    kv = lambda c: pl.BlockSpec((S, KV_W), lambda b, t: (b, qcols + c))
    const = lambda a: pl.BlockSpec(a.shape, lambda b, t: (0,) * a.ndim)
    return pl.pallas_call(
        _nsa_kernel,
        grid=(B, nq),
        in_specs=[pl.BlockSpec((TQ, NSA_WIDTH), tok),
                  pl.BlockSpec((TQ, LANES), tok),
                  pl.BlockSpec((None, n_cmp_pad, KV_W), lambda b, t: (b, 0, 0)),
                  pl.BlockSpec((None, KV_W, n_cmp_pad), lambda b, t: (b, 0, 0)),
                  kv(0), kv(1), kv(2), kv(3),
                  const(tsel), const(twin), const(tcmp), const(eind), const(agg_t), const(cind),
                  pl.BlockSpec(memory_space=pltpu.SMEM)],
        out_specs=pl.BlockSpec((TQ, NSA_WIDTH), tok),
        out_shape=jax.ShapeDtypeStruct((B * S, NSA_WIDTH), MXU_DT),
        scratch_shapes=[pltpu.VMEM((TQ + S, 2 * LANES), MXU_DT),
                        pltpu.VMEM((VROWS, TQ + S), MXU_DT),
                        pltpu.VMEM((WINDOW + S, 2 * LANES), MXU_DT),
                        pltpu.VMEM((VROWS, WINDOW + S), MXU_DT),
                        pltpu.VMEM((n_cmp_pad, 2 * LANES), MXU_DT),
                        pltpu.VMEM((n_cmp_pad, COLS), F32),
                        pltpu.VMEM((1, COLS), F32),
                        pltpu.VMEM((VROWS, COLS), F32),
                        pltpu.VMEM((LANES, COLS), F32),
                        pltpu.VMEM((LANES, COLS), F32),
                        pltpu.SMEM((2 * NSA_KV_GROUPS,), F32)],
        compiler_params=_cparams(("arbitrary", "arbitrary")),
        name="nsa_attention",
    )(proj_n, gates, kcmp, vcmp_t, proj_n, proj_n, proj_n, proj_n, tsel, twin, tcmp, eind, agg_t, cind,
      jnp.stack([jnp.max(tsel), jnp.max(twin)]))


def _layer_norm(x, g, b):
    mu = jnp.mean(x, axis=-1, keepdims=True)
    xc = x - mu
    var = jnp.mean(xc * xc, axis=-1, keepdims=True)
    return xc * lax.rsqrt(var + 1e-5) * g + b


def _post_kernel(x_ref, oa_ref, ob_ref, woa_ref, wob_ref, g1_ref, b1_ref, wg_ref, wu_ref, wd_ref, g2_ref, b2_ref,
                 o_ref, acc_ref):
    mix = _dot(oa_ref[...], woa_ref[...]) + _dot(ob_ref[...], wob_ref[...])
    x1 = _layer_norm(ALPHA * x_ref[...] + mix, g1_ref[...], b1_ref[...])
    xb = x1.astype(MXU_DT)
    tf = 256
    for c in range(0, D_FF, tf):
        gate = _dot(xb, wg_ref[:, c:c + tf])
        up = _dot(xb, wu_ref[:, c:c + tf])
        h = (gate / (1.0 + jnp.exp(-gate)) * up).astype(MXU_DT)
        d = _dot(h, wd_ref[c:c + tf, :])
        if c == 0:
            acc_ref[...] = d
        else:
            acc_ref[...] += d
    o_ref[...] = _layer_norm(ALPHA * x1 + acc_ref[...], g2_ref[...], b2_ref[...])


def _post(x2, oa, ob, woa, wob, g1, b1, wg, wu, wd, g2, b2, tm=512):
    T = x2.shape[0]
    row = lambda i: (i, 0)
    const = lambda a: pl.BlockSpec(a.shape, lambda i: (0, 0), pipeline_mode=pl.Buffered(1))
    return pl.pallas_call(
        _post_kernel,
        grid=(T // tm,),
        in_specs=[pl.BlockSpec((tm, D_MODEL), row),
                  pl.BlockSpec((tm, HG_WIDTH), row),
                  pl.BlockSpec((tm, NSA_WIDTH), row),
                  const(woa), const(wob), const(g1), const(b1),
                  const(wg), const(wu), const(wd), const(g2), const(b2)],
        out_specs=pl.BlockSpec((tm, D_MODEL), row),
        out_shape=jax.ShapeDtypeStruct((T, D_MODEL), F32),
        scratch_shapes=[pltpu.VMEM((tm, D_MODEL), F32)],
        compiler_params=_cparams(("parallel",)),
        name="outproj_ffn",
    )(x2, oa, ob, woa, wob, g1, b1, wg, wu, wd, g2, b2)


def _split_w_in(w):
    o = 4 * HG_WIDTH
    wh = w[:, :o]
    wq = w[:, o:o + NSA_WIDTH].reshape(D_MODEL, NSA_HEADS, NSA_HEAD_DIM)[:, list(HEAD_PERM)].reshape(D_MODEL, NSA_WIDTH)
    wq = wq * (NSA_HEAD_DIM ** -0.5 * LOG2E)
    o += NSA_WIDTH
    kc, vc, ks, vs, kw, vw = (w[:, o + i * KV_W:o + (i + 1) * KV_W] for i in range(6))
    o += 6 * KV_W
    wgt = jnp.pad(w[:, o:], ((0, 0), (0, LANES - 3 * NSA_HEADS)))
    wn = jnp.concatenate([wq, ks, vs, kw, vw], axis=1)
    wc = jnp.concatenate([kc, vc], axis=1)
    return wh.astype(MXU_DT), wn.astype(MXU_DT), wc.astype(MXU_DT), wgt.astype(MXU_DT)


def kernel(x, w_in, hg_lb_param, hg_norm_w, cmp_pos_k, cmp_w1_k, cmp_w2_k, cmp_pos_v, cmp_w1_v, cmp_w2_v,
           rel_bias, w_out, ln1_g, ln1_b, w_ffn_gate, w_ffn_up, w_ffn_down, ln2_g, ln2_b):
    B, S, _ = x.shape
    T = B * S
    depth = w_in.shape[0]

    p = jax.nn.softmax(hg_lb_param.astype(F32), axis=0)
    c = jnp.cumsum(p, axis=0)
    lbs = c - c[0:1]
    loglb, log1mlb, onemlb = jnp.log(lbs), jnp.log1p(-lbs), 1.0 - lbs

    amat_np, masks_np = _hgrn_consts()
    amat, masks = jnp.asarray(amat_np, MXU_DT), jnp.asarray(masks_np)
    eind, agg_t, cind = (jnp.asarray(a, MXU_DT) for a in _nsa_consts(S))
    tsel, twin, tcmp = _nsa_tables(rel_bias)
    out_perm = np.concatenate([np.arange(h * NSA_HEAD_DIM, (h + 1) * NSA_HEAD_DIM) for h in HEAD_PERM])

    x2 = x.reshape(T, D_MODEL).astype(F32)
    for l in range(depth):
        wh, wn, wc, wgt = _split_w_in(w_in[l])
        proj_h, proj_n, kc, vc, gates = _inproj(x2, wh, wn, wc, wgt)
        o_a = _hgrn(proj_h, loglb[l:l + 1], log1mlb[l:l + 1], onemlb[l:l + 1], hg_norm_w[l][None, :], amat, masks, B, S)
        shape3 = (B, S // CMP_STRIDE, CMP_STRIDE * KV_W)
        kcmp, vcmp_t = _compress(kc.reshape(shape3), vc.reshape(shape3),
                                 _compress_params(cmp_pos_k[l], cmp_w1_k[l], cmp_w2_k[l]),
                                 _compress_params(cmp_pos_v[l], cmp_w1_v[l], cmp_w2_v[l]))
        o_b = _nsa(proj_n, gates, kcmp, vcmp_t, tsel, twin, tcmp, eind, agg_t, cind, B, S)
        woa = w_out[l][:HG_WIDTH].astype(MXU_DT)
        wob = w_out[l][HG_WIDTH:][out_perm].astype(MXU_DT)
        x2 = _post(x2, o_a, o_b, woa, wob, ln1_g[l][None, :], ln1_b[l][None, :],
                   w_ffn_gate[l].astype(MXU_DT), w_ffn_up[l].astype(MXU_DT), w_ffn_down[l].astype(MXU_DT),
                   ln2_g[l][None, :], ln2_b[l][None, :])
    return x2.reshape(B, S, D_MODEL).astype(x.dtype)
```

```python
import functools
import math

import numpy as np
import jax
import jax.numpy as jnp
from jax import lax
from jax.experimental import pallas as pl
from jax.experimental.pallas import tpu as pltpu

F32 = jnp.float32
MXU_DT = jnp.bfloat16

D_MODEL = 1024
DEPTH = 4
HG_WIDTH = 512
HG_HEAD_DIM = 128
HG_HEADS = 4
HG_CHUNK = 64
NSA_WIDTH = 512
NSA_HEAD_DIM = 64
NSA_HEADS = 8
NSA_KV_GROUPS = 2
NSA_HPG = 4
CMP_BLOCK = 32
CMP_STRIDE = 16
CMP_HIDDEN = 128
SLC_BLOCK = 64
SLC_TOPK = 16
SLC_LOCAL = 2
WINDOW = 512
N_BUCKETS = 32
MAX_DISTANCE = 128
D_FF = 2816
ALPHA = (2.0 * DEPTH) ** 0.25
KV_W = NSA_KV_GROUPS * NSA_HEAD_DIM
LANES = 128
SUBLANES = 8
BIG = 1e9
MASK_NEG = -1e30
SEL_NEG = MASK_NEG
LOG2E = math.log2(math.e)

VMEM_LIMIT = 56 * 1024 * 1024

TQ = 128
COLS = NSA_HEADS * TQ
SEL_TK = 1024
BOUND_LIMIT = 40.0
BOUND_SLACK = 1.01
VROWS = LANES + 16
DIAG_W = 2 * TQ
WIN_W = WINDOW + TQ
CMP_BAND = 24
CMP_TAB = 40
HEAD_PERM = tuple(g * NSA_HPG + j for j in range(NSA_HPG) for g in range(NSA_KV_GROUPS))


def _dot(a, b):
    return jnp.dot(a, b, preferred_element_type=F32)


def _dot_nt(a, b):
    return lax.dot_general(a, b, (((1,), (1,)), ((), ())), preferred_element_type=F32)


def _dot_tn(a, b):
    return lax.dot_general(a, b, (((0,), (0,)), ((), ())), preferred_element_type=F32)


def _dot_split(x, w, parts, x_is_rhs=False):
    acc = None
    r = x
    for p in range(parts):
        h = r.astype(MXU_DT)
        d = _dot(w, h) if x_is_rhs else _dot(h, w)
        acc = d if acc is None else acc + d
        if p + 1 < parts:
            r = r - h.astype(F32)
    return acc


def _cparams(sem):
    return pltpu.CompilerParams(dimension_semantics=sem, vmem_limit_bytes=VMEM_LIMIT)


def _inproj_kernel(x_ref, wh_ref, wn_ref, wc_ref, wg_ref, oh_ref, on_ref, okc_ref, ovc_ref, og_ref):
    xb = x_ref[...].astype(MXU_DT)
    nh = oh_ref.shape[1]
    for c in range(0, nh, 512):
        oh_ref[:, c:c + 512] = _dot(xb, wh_ref[:, c:c + 512])
    nn = on_ref.shape[1]
    for c in range(0, nn, 512):
        w = min(512, nn - c)
        on_ref[:, c:c + w] = _dot(xb, wn_ref[:, c:c + w]).astype(on_ref.dtype)
    kv = _dot(xb, wc_ref[...])
    okc_ref[...] = kv[:, :KV_W].astype(okc_ref.dtype)
    ovc_ref[...] = kv[:, KV_W:].astype(ovc_ref.dtype)
    og_ref[...] = _dot(xb, wg_ref[...])


def _inproj(x2, wh, wn, wc, wg, layer, tm=512):
    T = x2.shape[0]
    nh, nn = wh.shape[2], wn.shape[2]
    full = lambda i: (layer, 0, 0)
    row = lambda i: (i, 0)
    return pl.pallas_call(
        _inproj_kernel,
        grid=(T // tm,),
        in_specs=[pl.BlockSpec((tm, D_MODEL), row),
                  pl.BlockSpec((None, D_MODEL, nh), full),
                  pl.BlockSpec((None, D_MODEL, nn), full),
                  pl.BlockSpec((None, D_MODEL, 2 * KV_W), full),
                  pl.BlockSpec((None, D_MODEL, LANES), full)],
        out_specs=[pl.BlockSpec((tm, nh), row),
                   pl.BlockSpec((tm, nn), row),
                   pl.BlockSpec((tm, KV_W), row),
                   pl.BlockSpec((tm, KV_W), row),
                   pl.BlockSpec((tm, LANES), row)],
        out_shape=[jax.ShapeDtypeStruct((T, nh), F32),
                   jax.ShapeDtypeStruct((T, nn), MXU_DT),
                   jax.ShapeDtypeStruct((T, KV_W), MXU_DT),
                   jax.ShapeDtypeStruct((T, KV_W), MXU_DT),
                   jax.ShapeDtypeStruct((T, LANES), F32)],
        compiler_params=_cparams(("parallel",)),
        name="inproj",
    )(x2, wh, wn, wc, wg)


_HG_LEVELS = (32, 16, 8, 4, 2, 1)
_HG_NLEV = len(_HG_LEVELS)
HG_FAST_LIMIT = -80.0
HG_PREP_ROWS = 16


def _hgrn_consts():
    C = HG_CHUNK
    a = np.zeros((8 * C, C), np.float32)
    m = np.zeros((_HG_NLEV + 1, C, C), np.float32)
    for l, h in enumerate(_HG_LEVELS):
        for r in range(C):
            mid = (r // (2 * h)) * 2 * h + h
            if r >= mid:
                a[l * C + r, mid:r + 1] = 1.0
                m[l, r, mid - h:mid] = 1.0
            else:
                a[l * C + r, r + 1:mid] = -1.0
    for r in range(C):
        a[6 * C + r, :r + 1] = 1.0
        a[7 * C + r, r + 1:] = 1.0
        m[_HG_NLEV, r, r] = 1.0
    return a, m


def _hgrn_kernel(q_ref, f_ref, i_ref, g_ref, loglb_ref, log1mlb_ref, onemlb_ref, gw_ref, amat_ref, mask_ref,
                 o_ref, state_ref, logf_s, k_s):
    C, Dk = HG_CHUNK, HG_HEAD_DIM
    ts = q_ref.shape[0]

    @pl.when(pl.program_id(1) == 0)
    def _():
        state_ref[...] = jnp.zeros_like(state_ref)

    a = loglb_ref[...]
    for r in range(0, ts, HG_PREP_ROWS):
        rows = slice(r, r + HG_PREP_ROWS)
        z = f_ref[rows, :]
        e = jnp.exp(-jnp.abs(z))
        e1 = 1.0 + e
        bt = log1mlb_ref[...] + (jnp.minimum(z, 0.0) - jnp.log(e1))
        logf_s[rows, :] = jnp.maximum(a, bt) + jnp.log(1.0 + jnp.exp(-jnp.abs(a - bt)))
        k_s[rows, :] = onemlb_ref[...] * jnp.where(z > 0.0, e, 1.0) / e1

    def finish(o, rows, cols):
        o = o * lax.rsqrt(jnp.mean(o * o, axis=-1, keepdims=True) + 1e-6) * gw_ref[...]
        g = g_ref[rows, cols]
        o_ref[rows, cols] = (o * (g / (1.0 + jnp.exp(-g)))).astype(o_ref.dtype)

    def robust():
        amat = amat_ref[...]

        def chunk(c, carry):
            r0 = pl.multiple_of(c * C, C)
            rows = pl.ds(r0, C)
            for h in range(HG_HEADS):
                cols = slice(h * Dk, (h + 1) * Dk)
                d = _dot_split(logf_s[rows, cols], amat, 2, x_is_rhs=True)
                q = q_ref[rows, cols]
                k = k_s[rows, cols]
                v = i_ref[rows, cols].astype(MXU_DT)
                sc = _dot_nt(q.astype(MXU_DT), k.astype(MXU_DT)) * mask_ref[_HG_NLEV]
                for l in range(_HG_NLEV):
                    el = jnp.exp(-jnp.abs(d[l * C:(l + 1) * C]))
                    sc = sc + _dot_nt((q * el).astype(MXU_DT), (k * el).astype(MXU_DT)) * mask_ref[l]
                b = d[6 * C:7 * C]
                st = state_ref[h]
                o = _dot(sc.astype(MXU_DT), v) + _dot_nt((q * jnp.exp(b)).astype(MXU_DT), st.astype(MXU_DT))
                kdec = (k * jnp.exp(d[7 * C:8 * C])).astype(MXU_DT)
                state_ref[h] = st * jnp.exp(b[C - 1:C]) + _dot_tn(v, kdec)
                finish(o, rows, cols)
            return carry

        lax.fori_loop(0, ts // C, chunk, 0)

    def fast():
        tril_mat = amat_ref[6 * C:7 * C, :]
        causal = lax.broadcasted_iota(jnp.int32, (C, C), 0) >= lax.broadcasted_iota(jnp.int32, (C, C), 1)
        states = [state_ref[h] for h in range(HG_HEADS)]
        for c in range(ts // C):
            rows = slice(c * C, (c + 1) * C)
            b = _dot_split(logf_s[rows, :], tril_mat, 2, x_is_rhs=True)
            e = jnp.exp(b)
            dec = e[C - 1:C]
            qe = (q_ref[rows, :] * e).astype(MXU_DT)
            kt = k_s[rows, :] * jnp.exp(-b)
            ktb = kt.astype(MXU_DT)
            kdec = (kt * dec).astype(MXU_DT)
            for h in range(HG_HEADS):
                cols = slice(h * Dk, (h + 1) * Dk)
                v = i_ref[rows, cols].astype(MXU_DT)
                sc = jnp.where(causal, _dot_nt(qe[:, cols], ktb[:, cols]), 0.0)
                o = _dot(sc.astype(MXU_DT), v) + _dot_nt(qe[:, cols], states[h].astype(MXU_DT))
                states[h] = states[h] * dec[:, cols] + _dot_tn(v, kdec[:, cols])
                finish(o, rows, cols)
        for h in range(HG_HEADS):
            state_ref[h] = states[h]

    lf = logf_s[...]
    total = jnp.min(jnp.concatenate(
        [jnp.sum(lf[c * C:(c + 1) * C], axis=0, keepdims=True) for c in range(ts // C)], axis=0))
    lax.cond(total >= HG_FAST_LIMIT, fast, robust)


def _hgrn(proj_h, loglb, log1mlb, onemlb, gw, amat, masks, B, S, ts=512):
    nt = S // ts
    sec = lambda s: (lambda b, t: (b * nt + t, s))
    vec = pl.BlockSpec((1, HG_WIDTH), lambda b, t: (0, 0))
    return pl.pallas_call(
        _hgrn_kernel,
        grid=(B, nt),
        in_specs=[pl.BlockSpec((ts, HG_WIDTH), sec(0)),
                  pl.BlockSpec((ts, HG_WIDTH), sec(1)),
                  pl.BlockSpec((ts, HG_WIDTH), sec(2)),
                  pl.BlockSpec((ts, HG_WIDTH), sec(3)),
                  vec, vec, vec,
                  pl.BlockSpec((1, HG_HEAD_DIM), lambda b, t: (0, 0)),
                  pl.BlockSpec(amat.shape, lambda b, t: (0, 0)),
                  pl.BlockSpec(masks.shape, lambda b, t: (0, 0, 0))],
        out_specs=pl.BlockSpec((ts, HG_WIDTH), lambda b, t: (b * nt + t, 0)),
        out_shape=jax.ShapeDtypeStruct((B * S, HG_WIDTH), MXU_DT),
        scratch_shapes=[pltpu.VMEM((HG_HEADS, HG_HEAD_DIM, HG_HEAD_DIM), F32),
                        pltpu.VMEM((ts, HG_WIDTH), F32),
                        pltpu.VMEM((ts, HG_WIDTH), F32)],
        compiler_params=_cparams(("arbitrary", "arbitrary")),
        name="hgrn2",
    )(proj_h, proj_h, proj_h, proj_h, loglb, log1mlb, onemlb, gw, amat, masks)


def _gelu_tanh(x):
    return 0.5 * x * (1.0 + jnp.tanh(math.sqrt(2.0 / math.pi) * (x + 0.044715 * (x * x * x))))


def _compress_kernel(kc_ref, vc_ref, pka_ref, pkb_ref, wka_ref, wkb_ref, wk2_ref,
                     pva_ref, pvb_ref, wva_ref, wvb_ref, wv2_ref, ok_ref, ov_ref):
    def one(x_ref, pa_ref, pb_ref, wa_ref, wb_ref, w2_ref):
        x = x_ref[...]
        n = x.shape[0]
        ha = _dot(x, wa_ref[...])
        hb = _dot(x, wb_ref[...])
        posw = _dot(pa_ref[...], wa_ref[...]) + _dot(pb_ref[...], wb_ref[...])
        h = ha + pltpu.roll(hb, n - 1, axis=0) + posw[0:1]
        return _dot(_gelu_tanh(h).astype(MXU_DT), w2_ref[...])

    ok_ref[...] = one(kc_ref, pka_ref, pkb_ref, wka_ref, wkb_ref, wk2_ref).astype(ok_ref.dtype)
    ov_ref[...] = one(vc_ref, pva_ref, pvb_ref, wva_ref, wvb_ref, wv2_ref).T.astype(ov_ref.dtype)


def _compress(kc3, vc3, kparams, vparams):
    B, n, w = kc3.shape
    x_spec = pl.BlockSpec((None, n, w), lambda b: (b, 0, 0))
    full = lambda a: pl.BlockSpec(a.shape, lambda b: (0,) * a.ndim)
    return pl.pallas_call(
        _compress_kernel,
        grid=(B,),
        in_specs=[x_spec, x_spec] + [full(a) for a in kparams] + [full(a) for a in vparams],
        out_specs=[pl.BlockSpec((None, n, KV_W), lambda b: (b, 0, 0)),
                   pl.BlockSpec((None, KV_W, n), lambda b: (b, 0, 0))],
        out_shape=[jax.ShapeDtypeStruct((B, n, KV_W), MXU_DT),
                   jax.ShapeDtypeStruct((B, KV_W, n), MXU_DT)],
        compiler_params=_cparams(("parallel",)),
        name="nsa_compress",
    )(kc3, vc3, *kparams, *vparams)


def _compress_params(pos, w1, w2):
    G, Dh, Hd = NSA_KV_GROUPS, NSA_HEAD_DIM, CMP_HIDDEN
    eye = jnp.eye(G, dtype=F32)

    def half(wh, ph):
        wbig = jnp.einsum('rdj,gk->rgdkj', wh.reshape(CMP_STRIDE, Dh, Hd), eye).reshape(CMP_STRIDE * G * Dh, G * Hd)
        pbig = jnp.broadcast_to(ph[:, None, :], (CMP_STRIDE, G, Dh)).reshape(1, CMP_STRIDE * G * Dh)
        return jnp.broadcast_to(pbig, (8, pbig.shape[1])).astype(MXU_DT), wbig.astype(MXU_DT)

    pa, wa = half(w1[:CMP_STRIDE * Dh], pos[:CMP_STRIDE])
    pb, wb = half(w1[CMP_STRIDE * Dh:], pos[CMP_STRIDE:])
    w2big = jnp.einsum('jd,gk->gjkd', w2, eye).reshape(G * Hd, G * Dh).astype(MXU_DT)
    return pa, pb, wa, wb, w2big


def _t5_bucket_np(n):
    n = np.maximum(n, 0)
    max_exact = N_BUCKETS // 2
    nf = np.maximum(n, 1).astype(np.float32)
    large = max_exact + (np.log(nf / np.float32(max_exact)) / np.float32(math.log(MAX_DISTANCE / max_exact))
                         * np.float32(N_BUCKETS - max_exact)).astype(np.int32)
    large = np.minimum(large, N_BUCKETS - 1)
    return np.where(n < max_exact, n, large).astype(np.int32)


def _nsa_tables(rel_bias):
    rb = rel_bias.astype(F32)

    def per_distance(dist, valid, rel_to_far):
        b = rb[jnp.asarray(_t5_bucket_np(dist))]
        if rel_to_far:
            b = b - rb[N_BUCKETS - 1]
        return (jnp.where(jnp.asarray(valid)[:, None], b, MASK_NEG) * LOG2E).T

    def rows_of(f, offsets, width):
        return jnp.stack([f[:, o:o + width] for o in offsets]).reshape(len(offsets), NSA_HEADS * width)

    def key_major(n_keys, first_dist, valid, rel_to_far):
        d = first_dist + TQ - 1 - np.arange(n_keys + TQ - 1)
        f = per_distance(d, valid(d), rel_to_far)
        t = jnp.stack([f[:, TQ - 1 - i:TQ - 1 - i + n_keys] for i in range(TQ)])
        return jnp.transpose(t, (2, 1, 0)).reshape(n_keys, COLS)

    t_sel = key_major(DIAG_W, TQ, lambda d: d >= 0, True)
    t_win = key_major(WIN_W, WINDOW, lambda d: (d >= 0) & (d < WINDOW), False)
    d0 = -CMP_STRIDE * (CMP_TAB - 1 - 16) - (CMP_BLOCK - 1)
    d = d0 + np.arange(CMP_STRIDE * (CMP_TAB - 1) + TQ)
    t_cmp = rows_of(per_distance(d, d >= 0, True), [CMP_STRIDE * (CMP_TAB - 1 - r) for r in range(CMP_TAB)], TQ)
    return t_sel, t_win, t_cmp


def _nsa_consts(S):
    n_slc = S // SLC_BLOCK
    n_cmp_pad = S // CMP_STRIDE
    assert n_slc < LANES and n_cmp_pad >= CMP_BAND
    far = _t5_bucket_np(np.arange(8 * MAX_DISTANCE)) == N_BUCKETS - 1
    far_dist = int(np.max(np.nonzero(~far)[0])) + 1
    assert TQ + 1 >= far_dist and 17 * CMP_STRIDE - (CMP_BLOCK - 1) >= far_dist
    eind = np.zeros((S, LANES), np.float32)
    eind[np.arange(S), np.arange(S) // SLC_BLOCK] = 1.0
    cs = np.arange(n_cmp_pad)[None, :] * CMP_STRIDE
    ss = np.arange(LANES)[:, None] * SLC_BLOCK
    ov = np.clip(np.minimum(cs + CMP_BLOCK, ss + SLC_BLOCK) - np.maximum(cs, ss), 0, None)
    agg_t = (ov / CMP_STRIDE).astype(np.float32)
    agg_t[:, n_cmp_pad - 1] = 0.0
    agg_t[n_slc:] = 0.0
    assert n_cmp_pad <= SUBLANES * LANES
    cind = np.zeros((n_cmp_pad, LANES), np.float32)
    cind[np.arange(n_cmp_pad), np.arange(n_cmp_pad) // SUBLANES] = 1.0
    return eind, agg_t, cind


def _nsa_kernel(q_ref, gt_ref, kc_ref, vct_ref, ks_ref, vs_ref, kw_ref, vw_ref,
                tsel_ref, twin_ref, tcmp_ref, eind_ref, aggt_ref, cind_ref, tmax_ref,
                o_ref, kaug_s, vselt_s, kwin_s, vwint_s, kcaug_s, sc_s, m_s, acc_s, osel_s, owin_s, knorm_s):
    S = ks_ref.shape[0]
    n_slc = S // SLC_BLOCK
    n_cmp_pad = kc_ref.shape[0]
    qt = pl.program_id(1)
    t0 = pl.multiple_of(qt * TQ, TQ)

    @pl.when(qt == 0)
    def _():
        lane1 = lax.broadcasted_iota(jnp.int32, (1, LANES), 1)
        kaug_s[0:TQ, 0:LANES] = jnp.zeros((TQ, LANES), MXU_DT)
        kaug_s[0:TQ, LANES:] = jnp.broadcast_to(jnp.where(lane1 == LANES - 1, 1.0, 0.0).astype(MXU_DT), (TQ, LANES))
        kaug_s[TQ:, 0:LANES] = ks_ref[...]
        kaug_s[TQ:, LANES:] = eind_ref[...]
        kcaug_s[:, 0:LANES] = kc_ref[...]
        kcaug_s[:, LANES:] = cind_ref[...]
        rg = lax.broadcasted_iota(jnp.int32, (LANES, LANES), 0) // NSA_HEAD_DIM
        cg = lax.broadcasted_iota(jnp.int32, (LANES, LANES), 1) // NSA_HEAD_DIM
        same_group = jnp.where(rg == cg, 1.0, 0.0).astype(MXU_DT)
        for i, ref in enumerate((ks_ref, kw_ref)):
            n2 = _dot(jnp.square(ref[...].astype(F32)).astype(MXU_DT), same_group)
            n2 = jnp.max(n2, axis=0, keepdims=True)
            for g in range(NSA_KV_GROUPS):
                knorm_s[i * NSA_KV_GROUPS + g] = jnp.sqrt(jnp.max(n2[:, g * NSA_HEAD_DIM:(g + 1) * NSA_HEAD_DIM]))
        ones_rows = (lax.broadcasted_iota(jnp.int32, (VROWS - LANES, 1), 0) == 0).astype(MXU_DT)
        vselt_s[0:LANES, 0:TQ] = jnp.zeros((LANES, TQ), MXU_DT)
        vselt_s[0:LANES, TQ:] = vs_ref[...].astype(F32).T.astype(MXU_DT)
        vselt_s[LANES:, :] = jnp.broadcast_to(ones_rows, (VROWS - LANES, TQ + S))
        kwin_s[0:WINDOW, 0:LANES] = jnp.zeros((WINDOW, LANES), MXU_DT)
        kwin_s[0:WINDOW, LANES:] = jnp.broadcast_to(jnp.where(lane1 == 0, 1.0, 0.0).astype(MXU_DT), (WINDOW, LANES))
        kwin_s[WINDOW:, 0:LANES] = kw_ref[...]
        kwin_s[WINDOW:, LANES:] = jnp.zeros((S, LANES), MXU_DT)
        vwint_s[0:LANES, 0:WINDOW] = jnp.zeros((LANES, WINDOW), MXU_DT)
        vwint_s[0:LANES, WINDOW:] = vw_ref[...].astype(F32).T.astype(MXU_DT)
        vwint_s[LANES:, :] = jnp.broadcast_to(ones_rows, (VROWS - LANES, WINDOW + S))

    lane = lax.broadcasted_iota(jnp.int32, (1, LANES), 1)
    lo_lane = lane < NSA_HEAD_DIM
    zero = jnp.zeros((), MXU_DT)

    qg = jnp.concatenate([jnp.where(lo_lane if g == 0 else jnp.logical_not(lo_lane),
                                    q_ref[:, j * LANES:(j + 1) * LANES], zero)
                          for g in range(NSA_KV_GROUPS) for j in range(NSA_HPG)], axis=0)

    n0 = pl.multiple_of(qt * (TQ // CMP_STRIDE), SUBLANES)
    late = jnp.where(lane > n0 // SUBLANES, MASK_NEG, 0.0).astype(MXU_DT)
    sc_s[...] = _dot_nt(kcaug_s[...], jnp.concatenate([qg, jnp.broadcast_to(late, (COLS, LANES))], axis=1))
    band = pl.multiple_of(jnp.maximum(n0 - 16, 0), SUBLANES)
    toff = pl.multiple_of(16 + band - n0, SUBLANES)
    sc_s[pl.ds(band, CMP_BAND), :] += tcmp_ref[pl.ds(toff, CMP_BAND), :]
    s_c = sc_s[...]
    m_c = jnp.max(s_c, axis=0, keepdims=True)
    p_c = jnp.exp2(s_c - m_c)
    l_c = jnp.sum(p_c, axis=0, keepdims=True)
    p_c = p_c * jnp.where(m_c > 0.5 * MASK_NEG, 1.0 / l_c, 0.0)
    o_c = _dot(vct_ref[...], p_c.astype(MXU_DT))

    p_sum = jnp.concatenate(
        [sum(p_c[:, (g * NSA_HPG + j) * TQ:(g * NSA_HPG + j + 1) * TQ] for j in range(NSA_HPG))
         for g in range(NSA_KV_GROUPS)], axis=1)
    n_rows = -(-n_slc // SUBLANES) * SUBLANES
    imp = _dot_split(p_sum, aggt_ref[...], 3, x_is_rhs=True)[0:n_rows]
    blk = lax.broadcasted_iota(jnp.int32, (n_rows, 1), 0)
    blk_f = blk.astype(F32)
    tq_pos = t0 + lax.broadcasted_iota(jnp.int32, (1, TQ), 1)
    tpos = jnp.concatenate([tq_pos] * NSA_KV_GROUPS, axis=1)
    diff = tpos // SLC_BLOCK - blk
    forced = (blk == 0) | ((diff >= 0) & (diff < SLC_LOCAL))
    score = jnp.where(forced, -jnp.inf, jnp.where(blk * SLC_BLOCK <= tpos, imp, -BIG))
    score = jnp.where(blk < n_slc, score, -jnp.inf)
    sel = jnp.where(forced, 1.0, 0.0)
    for _ in range(max(min(SLC_TOPK, n_slc) - (1 + SLC_LOCAL), 0)):
        mx = jnp.max(score, axis=0, keepdims=True)
        idx = jnp.min(jnp.where(score == mx, blk_f, float(LANES)), axis=0, keepdims=True)
        hit = blk_f == idx
        sel = jnp.where(hit, 1.0, sel)
        score = jnp.where(hit, -jnp.inf, score)
    if n_rows < LANES:
        sel = jnp.concatenate([sel, jnp.zeros((LANES - n_rows, sel.shape[1]), F32)], axis=0)
    selb = (sel - 1.0) * (-SEL_NEG)
    selb_t = [selb[:, g * TQ:(g + 1) * TQ].T for g in range(NSA_KV_GROUPS)]
    selb_rows = jnp.concatenate([selb_t[g] for g in range(NSA_KV_GROUPS) for _ in range(NSA_HPG)], axis=0)

    qn = jnp.sqrt(_dot_nt(jnp.ones((SUBLANES, LANES), MXU_DT), jnp.square(qg.astype(F32)).astype(MXU_DT))[0:1])
    first_group = lax.broadcasted_iota(jnp.int32, (1, COLS), 1) < COLS // NSA_KV_GROUPS
    b_sel = qn * jnp.where(first_group, knorm_s[0], knorm_s[1]) * BOUND_SLACK + (tmax_ref[0] + 1.0)
    b_win = qn * jnp.where(first_group, knorm_s[2], knorm_s[3]) * BOUND_SLACK + (tmax_ref[1] + 1.0)
    bounded = jnp.maximum(jnp.max(b_sel), jnp.max(b_win)) <= BOUND_LIMIT
    qa = jnp.concatenate([qg, selb_rows.astype(MXU_DT)], axis=1)
    qw = jnp.concatenate(
        [qg, jnp.broadcast_to(jnp.where(lane == 0, SEL_NEG, 0.0).astype(MXU_DT), (COLS, LANES))], axis=1)

    def attend(use_bound):
        if not use_bound:
            m_s[...] = jnp.full(m_s.shape, MASK_NEG, F32)
        acc_s[...] = jnp.zeros(acc_s.shape, F32)

        def update(start, width, table_ref=None):
            s = _dot_nt(kaug_s[pl.ds(start, width), :], qa)
            if table_ref is not None:
                s = s + table_ref[...]
            vt = vselt_s[:, pl.ds(start, width)]
            if use_bound:
                acc_s[...] += _dot(vt, jnp.exp2((s - b_sel).astype(MXU_DT)))
            else:
                m_prev = m_s[...]
                m_new = jnp.maximum(m_prev, jnp.max(s, axis=0, keepdims=True))
                p = jnp.exp2((s - m_new).astype(MXU_DT))
                acc_s[...] = jnp.exp2(m_prev - m_new) * acc_s[...] + _dot(vt, p)
                m_s[...] = m_new

        def far_tile(i, carry):
            update(pl.multiple_of(TQ + i * SEL_TK, LANES), SEL_TK)
            return carry

        n_far = jnp.maximum(t0 - TQ, 0)
        n_full = n_far // SEL_TK
        lax.fori_loop(0, n_full, far_tile, 0)
        pos = n_full * SEL_TK
        width = SEL_TK // 2
        while width >= LANES:
            digit = ((n_far - n_full * SEL_TK) & width) != 0
            pl.when(digit)(functools.partial(update, pl.multiple_of(TQ + pos, LANES), width))
            pos = pos + jnp.where(digit, width, 0)
            width //= 2
        update(t0, DIAG_W, tsel_ref)
        acc = acc_s[...]
        osel_s[...] = acc[0:LANES] * (1.0 / acc[LANES:LANES + 1])

        s_w = _dot_nt(kwin_s[pl.ds(t0, WIN_W), :], qw) + twin_ref[...]
        s_w = s_w - (b_win if use_bound else jnp.max(s_w, axis=0, keepdims=True))
        acc = _dot(vwint_s[:, pl.ds(t0, WIN_W)], jnp.exp2(s_w.astype(MXU_DT)))
        owin_s[...] = acc[0:LANES] * (1.0 / acc[LANES:LANES + 1])

    lax.cond(bounded, lambda: attend(True), lambda: attend(False))
    o_s = osel_s[...]
    o_w = owin_s[...]

    gsig_t = (1.0 / (1.0 + jnp.exp(-gt_ref[...]))).T
    row_lo = lax.broadcasted_iota(jnp.int32, (LANES, 1), 0) < NSA_HEAD_DIM
    for j in range(NSA_HPG):
        c0, c1 = j * TQ, (NSA_HPG + j) * TQ
        acc = None
        for br, o in enumerate((o_c, o_s, o_w)):
            r0, r1 = br * NSA_HEADS + j, br * NSA_HEADS + NSA_HPG + j
            gate = jnp.where(row_lo, gsig_t[r0:r0 + 1, :], gsig_t[r1:r1 + 1, :])
            term = gate * jnp.where(row_lo, o[:, c0:c0 + TQ], o[:, c1:c1 + TQ])
            acc = term if acc is None else acc + term
        o_ref[:, j * LANES:(j + 1) * LANES] = acc.T.astype(o_ref.dtype)


def _nsa(proj_n, gates, kcmp, vcmp_t, tsel, twin, tcmp, eind, agg_t, cind, B, S):
    nq = S // TQ
    n_cmp_pad = S // CMP_STRIDE
    qcols = NSA_WIDTH // LANES
    tok = lambda b, t: (b * nq + t, 0)
    kv = lambda c: pl.BlockSpec((S, KV_W), lambda b, t: (b, qcols + c))
    const = lambda a: pl.BlockSpec(a.shape, lambda b, t: (0,) * a.ndim)
    return pl.pallas_call(
        _nsa_kernel,
        grid=(B, nq),
        in_specs=[pl.BlockSpec((TQ, NSA_WIDTH), tok),
                  pl.BlockSpec((TQ, LANES), tok),
                  pl.BlockSpec((None, n_cmp_pad, KV_W), lambda b, t: (b, 0, 0)),
                  pl.BlockSpec((None, KV_W, n_cmp_pad), lambda b, t: (b, 0, 0)),
                  kv(0), kv(1), kv(2), kv(3),
                  const(tsel), const(twin), const(tcmp), const(eind), const(agg_t), const(cind),
                  pl.BlockSpec(memory_space=pltpu.SMEM)],
        out_specs=pl.BlockSpec((TQ, NSA_WIDTH), tok),
        out_shape=jax.ShapeDtypeStruct((B * S, NSA_WIDTH), MXU_DT),
        scratch_shapes=[pltpu.VMEM((TQ + S, 2 * LANES), MXU_DT),
                        pltpu.VMEM((VROWS, TQ + S), MXU_DT),
                        pltpu.VMEM((WINDOW + S, 2 * LANES), MXU_DT),
                        pltpu.VMEM((VROWS, WINDOW + S), MXU_DT),
                        pltpu.VMEM((n_cmp_pad, 2 * LANES), MXU_DT),
                        pltpu.VMEM((n_cmp_pad, COLS), F32),
                        pltpu.VMEM((1, COLS), F32),
                        pltpu.VMEM((VROWS, COLS), F32),
                        pltpu.VMEM((LANES, COLS), F32),
                        pltpu.VMEM((LANES, COLS), F32),
                        pltpu.SMEM((2 * NSA_KV_GROUPS,), F32)],
        compiler_params=_cparams(("arbitrary", "arbitrary")),
        name="nsa_attention",
    )(proj_n, gates, kcmp, vcmp_t, proj_n, proj_n, proj_n, proj_n, tsel, twin, tcmp, eind, agg_t, cind,
      jnp.stack([jnp.max(tsel), jnp.max(twin)]))


def _layer_norm(x, g, b):
    mu = jnp.mean(x, axis=-1, keepdims=True)
    xc = x - mu
    var = jnp.mean(xc * xc, axis=-1, keepdims=True)
    return xc * lax.rsqrt(var + 1e-5) * g + b


def _post_kernel(x_ref, oa_ref, ob_ref, woa_ref, wob_ref, g1_ref, b1_ref, wg_ref, wu_ref, wd_ref, g2_ref, b2_ref,
                 o_ref, acc_ref):
    mix = _dot(oa_ref[...], woa_ref[...]) + _dot(ob_ref[...], wob_ref[...])
    x1 = _layer_norm(ALPHA * x_ref[...] + mix, g1_ref[...], b1_ref[...])
    xb = x1.astype(MXU_DT)
    tf = 256
    for c in range(0, D_FF, tf):
        gate = _dot(xb, wg_ref[:, c:c + tf])
        up = _dot(xb, wu_ref[:, c:c + tf])
        h = (gate / (1.0 + jnp.exp(-gate)) * up).astype(MXU_DT)
        d = _dot(h, wd_ref[c:c + tf, :])
        if c == 0:
            acc_ref[...] = d
        else:
            acc_ref[...] += d
    o_ref[...] = _layer_norm(ALPHA * x1 + acc_ref[...], g2_ref[...], b2_ref[...])


def _post(x2, oa, ob, woa, wob, g1, b1, wg, wu, wd, g2, b2, layer, tm=512):
    T = x2.shape[0]
    row = lambda i: (i, 0)
    const = lambda a: pl.BlockSpec((None,) + a.shape[1:], lambda i: (layer, 0, 0), pipeline_mode=pl.Buffered(1))
    return pl.pallas_call(
        _post_kernel,
        grid=(T // tm,),
        in_specs=[pl.BlockSpec((tm, D_MODEL), row),
                  pl.BlockSpec((tm, HG_WIDTH), row),
                  pl.BlockSpec((tm, NSA_WIDTH), row),
                  const(woa), const(wob), const(g1), const(b1),
                  const(wg), const(wu), const(wd), const(g2), const(b2)],
        out_specs=pl.BlockSpec((tm, D_MODEL), row),
        out_shape=jax.ShapeDtypeStruct((T, D_MODEL), F32),
        scratch_shapes=[pltpu.VMEM((tm, D_MODEL), F32)],
        compiler_params=_cparams(("parallel",)),
        name="outproj_ffn",
    )(x2, oa, ob, woa, wob, g1, b1, wg, wu, wd, g2, b2)


def _split_w_in(w):
    depth = w.shape[0]
    o = 4 * HG_WIDTH
    wh = w[:, :, :o]
    wq = w[:, :, o:o + NSA_WIDTH].reshape(depth, D_MODEL, NSA_KV_GROUPS, NSA_HPG, NSA_HEAD_DIM)
    wq = jnp.swapaxes(wq, 2, 3).reshape(depth, D_MODEL, NSA_WIDTH)
    wq = wq * (NSA_HEAD_DIM ** -0.5 * LOG2E)
    o += NSA_WIDTH
    kc, vc, ks, vs, kw, vw = (w[:, :, o + i * KV_W:o + (i + 1) * KV_W] for i in range(6))
    o += 6 * KV_W
    wgt = jnp.pad(w[:, :, o:], ((0, 0), (0, 0), (0, LANES - 3 * NSA_HEADS)))
    wn = jnp.concatenate([wq, ks, vs, kw, vw], axis=2)
    wc = jnp.concatenate([kc, vc], axis=2)
    return wh.astype(MXU_DT), wn.astype(MXU_DT), wc.astype(MXU_DT), wgt.astype(MXU_DT)


def kernel(x, w_in, hg_lb_param, hg_norm_w, cmp_pos_k, cmp_w1_k, cmp_w2_k, cmp_pos_v, cmp_w1_v, cmp_w2_v,
           rel_bias, w_out, ln1_g, ln1_b, w_ffn_gate, w_ffn_up, w_ffn_down, ln2_g, ln2_b):
    B, S, _ = x.shape
    T = B * S
    depth = w_in.shape[0]

    p = jax.nn.softmax(hg_lb_param.astype(F32), axis=0)
    c = jnp.cumsum(p, axis=0)
    lbs = c - c[0:1]
    loglb, log1mlb, onemlb = jnp.log(lbs), jnp.log1p(-lbs), 1.0 - lbs

    amat_np, masks_np = _hgrn_consts()
    amat, masks = jnp.asarray(amat_np, MXU_DT), jnp.asarray(masks_np)
    eind, agg_t, cind = (jnp.asarray(a, MXU_DT) for a in _nsa_consts(S))
    tsel, twin, tcmp = _nsa_tables(rel_bias)

    wh, wn, wc, wgt = _split_w_in(w_in)
    woa = w_out[:, :HG_WIDTH].astype(MXU_DT)
    wob = w_out[:, HG_WIDTH:].reshape(depth, NSA_KV_GROUPS, NSA_HPG, NSA_HEAD_DIM, D_MODEL)
    wob = jnp.swapaxes(wob, 1, 2).reshape(depth, NSA_WIDTH, D_MODEL).astype(MXU_DT)
    wfg, wfu, wfd = w_ffn_gate.astype(MXU_DT), w_ffn_up.astype(MXU_DT), w_ffn_down.astype(MXU_DT)
    ln = [a[:, None, :] for a in (ln1_g, ln1_b, ln2_g, ln2_b)]

    x2 = x.reshape(T, D_MODEL).astype(F32)
    for l in range(depth):
        proj_h, proj_n, kc, vc, gates = _inproj(x2, wh, wn, wc, wgt, l)
        o_a = _hgrn(proj_h, loglb[l:l + 1], log1mlb[l:l + 1], onemlb[l:l + 1], hg_norm_w[l][None, :], amat, masks, B, S)
        shape3 = (B, S // CMP_STRIDE, CMP_STRIDE * KV_W)
        kcmp, vcmp_t = _compress(kc.reshape(shape3), vc.reshape(shape3),
                                 _compress_params(cmp_pos_k[l], cmp_w1_k[l], cmp_w2_k[l]),
                                 _compress_params(cmp_pos_v[l], cmp_w1_v[l], cmp_w2_v[l]))
        o_b = _nsa(proj_n, gates, kcmp, vcmp_t, tsel, twin, tcmp, eind, agg_t, cind, B, S)
        x2 = _post(x2, o_a, o_b, woa, wob, ln[0], ln[1], wfg, wfu, wfd, ln[2], ln[3], l)
    return x2.reshape(B, S, D_MODEL).astype(x.dtype)
```

```python
import functools
import math

import numpy as np
import jax
import jax.numpy as jnp
from jax import lax
from jax.experimental import pallas as pl
from jax.experimental.pallas import tpu as pltpu

F32 = jnp.float32
MXU_DT = jnp.bfloat16

D_MODEL = 1024
DEPTH = 4
HG_WIDTH = 512
HG_HEAD_DIM = 128
HG_HEADS = 4
HG_CHUNK = 64
NSA_WIDTH = 512
NSA_HEAD_DIM = 64
NSA_HEADS = 8
NSA_KV_GROUPS = 2
NSA_HPG = 4
CMP_BLOCK = 32
CMP_STRIDE = 16
CMP_HIDDEN = 128
SLC_BLOCK = 64
SLC_TOPK = 16
SLC_LOCAL = 2
WINDOW = 512
N_BUCKETS = 32
MAX_DISTANCE = 128
D_FF = 2816
ALPHA = (2.0 * DEPTH) ** 0.25
KV_W = NSA_KV_GROUPS * NSA_HEAD_DIM
LANES = 128
SUBLANES = 8
BIG = 1e9
MASK_NEG = -1e30
SEL_NEG = MASK_NEG
LOG2E = math.log2(math.e)

VMEM_LIMIT = 56 * 1024 * 1024

TQ = 128
COLS = NSA_HEADS * TQ
SEL_TK = 1024
BOUND_LIMIT = 40.0
BOUND_SLACK = 1.01
VROWS = LANES + 16
DIAG_W = 2 * TQ
WIN_W = WINDOW + TQ
CMP_BAND = 24
CMP_TAB = 40
HEAD_PERM = tuple(g * NSA_HPG + j for j in range(NSA_HPG) for g in range(NSA_KV_GROUPS))


def _dot(a, b):
    return jnp.dot(a, b, preferred_element_type=F32)


def _dot_nt(a, b):
    return lax.dot_general(a, b, (((1,), (1,)), ((), ())), preferred_element_type=F32)


def _dot_tn(a, b):
    return lax.dot_general(a, b, (((0,), (0,)), ((), ())), preferred_element_type=F32)


def _dot_split(x, w, parts, x_is_rhs=False):
    acc = None
    r = x
    for p in range(parts):
        h = r.astype(MXU_DT)
        d = _dot(w, h) if x_is_rhs else _dot(h, w)
        acc = d if acc is None else acc + d
        if p + 1 < parts:
            r = r - h.astype(F32)
    return acc


def _cparams(sem):
    return pltpu.CompilerParams(dimension_semantics=sem, vmem_limit_bytes=VMEM_LIMIT)


def _inproj_kernel(x_ref, wh_ref, wn_ref, wc_ref, wg_ref, oh_ref, on_ref, okc_ref, ovc_ref, og_ref):
    xb = x_ref[...].astype(MXU_DT)
    nh = oh_ref.shape[1]
    for c in range(0, nh, 512):
        oh_ref[:, c:c + 512] = _dot(xb, wh_ref[:, c:c + 512])
    nn = on_ref.shape[1]
    for c in range(0, nn, 512):
        w = min(512, nn - c)
        on_ref[:, c:c + w] = _dot(xb, wn_ref[:, c:c + w]).astype(on_ref.dtype)
    kv = _dot(xb, wc_ref[...])
    okc_ref[...] = kv[:, :KV_W].astype(okc_ref.dtype)
    ovc_ref[...] = kv[:, KV_W:].astype(ovc_ref.dtype)
    og_ref[...] = _dot(xb, wg_ref[...])


def _inproj(x2, wh, wn, wc, wg, layer, tm=512):
    T = x2.shape[0]
    nh, nn = wh.shape[2], wn.shape[2]
    full = lambda i: (layer, 0, 0)
    row = lambda i: (i, 0)
    return pl.pallas_call(
        _inproj_kernel,
        grid=(T // tm,),
        in_specs=[pl.BlockSpec((tm, D_MODEL), row),
                  pl.BlockSpec((None, D_MODEL, nh), full),
                  pl.BlockSpec((None, D_MODEL, nn), full),
                  pl.BlockSpec((None, D_MODEL, 2 * KV_W), full),
                  pl.BlockSpec((None, D_MODEL, LANES), full)],
        out_specs=[pl.BlockSpec((tm, nh), row),
                   pl.BlockSpec((tm, nn), row),
                   pl.BlockSpec((tm, KV_W), row),
                   pl.BlockSpec((tm, KV_W), row),
                   pl.BlockSpec((tm, LANES), row)],
        out_shape=[jax.ShapeDtypeStruct((T, nh), F32),
                   jax.ShapeDtypeStruct((T, nn), MXU_DT),
                   jax.ShapeDtypeStruct((T, KV_W), MXU_DT),
                   jax.ShapeDtypeStruct((T, KV_W), MXU_DT),
                   jax.ShapeDtypeStruct((T, LANES), F32)],
        compiler_params=_cparams(("parallel",)),
        name="inproj",
    )(x2, wh, wn, wc, wg)


_HG_LEVELS = (32, 16, 8, 4, 2, 1)
_HG_NLEV = len(_HG_LEVELS)
HG_FAST_LIMIT = -80.0
HG_PREP_ROWS = 16


def _hgrn_consts():
    C = HG_CHUNK
    a = np.zeros((8 * C, C), np.float32)
    m = np.zeros((_HG_NLEV + 1, C, C), np.float32)
    for l, h in enumerate(_HG_LEVELS):
        for r in range(C):
            mid = (r // (2 * h)) * 2 * h + h
            if r >= mid:
                a[l * C + r, mid:r + 1] = 1.0
                m[l, r, mid - h:mid] = 1.0
            else:
                a[l * C + r, r + 1:mid] = -1.0
    for r in range(C):
        a[6 * C + r, :r + 1] = 1.0
        a[7 * C + r, r + 1:] = 1.0
        m[_HG_NLEV, r, r] = 1.0
    return a, m


def _hgrn_kernel(q_ref, f_ref, i_ref, g_ref, loglb_ref, log1mlb_ref, onemlb_ref, gw_ref, amat_ref, mask_ref,
                 o_ref, state_ref, logf_s, k_s):
    C, Dk = HG_CHUNK, HG_HEAD_DIM
    ts = q_ref.shape[0]

    @pl.when(pl.program_id(1) == 0)
    def _():
        state_ref[...] = jnp.zeros_like(state_ref)

    a = loglb_ref[...]
    for r in range(0, ts, HG_PREP_ROWS):
        rows = slice(r, r + HG_PREP_ROWS)
        z = f_ref[rows, :]
        e = jnp.exp(-jnp.abs(z))
        e1 = 1.0 + e
        bt = log1mlb_ref[...] + (jnp.minimum(z, 0.0) - jnp.log(e1))
        logf_s[rows, :] = jnp.maximum(a, bt) + jnp.log(1.0 + jnp.exp(-jnp.abs(a - bt)))
        k_s[rows, :] = onemlb_ref[...] * jnp.where(z > 0.0, e, 1.0) / e1

    def finish(o, rows, cols):
        o = o * lax.rsqrt(jnp.mean(o * o, axis=-1, keepdims=True) + 1e-6) * gw_ref[...]
        g = g_ref[rows, cols]
        o_ref[rows, cols] = (o * (g / (1.0 + jnp.exp(-g)))).astype(o_ref.dtype)

    def robust():
        amat = amat_ref[...]

        def chunk(c, carry):
            r0 = pl.multiple_of(c * C, C)
            rows = pl.ds(r0, C)
            for h in range(HG_HEADS):
                cols = slice(h * Dk, (h + 1) * Dk)
                d = _dot_split(logf_s[rows, cols], amat, 2, x_is_rhs=True)
                q = q_ref[rows, cols]
                k = k_s[rows, cols]
                v = i_ref[rows, cols].astype(MXU_DT)
                sc = _dot_nt(q.astype(MXU_DT), k.astype(MXU_DT)) * mask_ref[_HG_NLEV]
                for l in range(_HG_NLEV):
                    el = jnp.exp(-jnp.abs(d[l * C:(l + 1) * C]))
                    sc = sc + _dot_nt((q * el).astype(MXU_DT), (k * el).astype(MXU_DT)) * mask_ref[l]
                b = d[6 * C:7 * C]
                st = state_ref[h]
                o = _dot(sc.astype(MXU_DT), v) + _dot_nt((q * jnp.exp(b)).astype(MXU_DT), st.astype(MXU_DT))
                kdec = (k * jnp.exp(d[7 * C:8 * C])).astype(MXU_DT)
                state_ref[h] = st * jnp.exp(b[C - 1:C]) + _dot_tn(v, kdec)
                finish(o, rows, cols)
            return carry

        lax.fori_loop(0, ts // C, chunk, 0)

    def fast():
        tril_mat = amat_ref[6 * C:7 * C, :]
        causal = lax.broadcasted_iota(jnp.int32, (C, C), 0) >= lax.broadcasted_iota(jnp.int32, (C, C), 1)
        n = ts // C
        chunks = [slice(c * C, (c + 1) * C) for c in range(n)]
        heads = [slice(h * Dk, (h + 1) * Dk) for h in range(HG_HEADS)]
        b = [_dot_split(logf_s[r, :], tril_mat, 2, x_is_rhs=True) for r in chunks]
        e = [jnp.exp(x) for x in b]
        dec = [x[C - 1:C] for x in e]
        qe = [(q_ref[r, :] * x).astype(MXU_DT) for r, x in zip(chunks, e)]
        kt = [k_s[r, :] * jnp.exp(-x) for r, x in zip(chunks, b)]
        ktb = [x.astype(MXU_DT) for x in kt]
        kdec = [(x * d).astype(MXU_DT) for x, d in zip(kt, dec)]
        v = [i_ref[r, :].astype(MXU_DT) for r in chunks]
        o_intra = [[_dot(jnp.where(causal, _dot_nt(qe[c][:, h], ktb[c][:, h]), 0.0).astype(MXU_DT), v[c][:, h])
                    for h in heads] for c in range(n)]
        upd = [[_dot_tn(v[c][:, h], kdec[c][:, h]) for h in heads] for c in range(n)]
        states = [[state_ref[i] for i in range(HG_HEADS)]]
        for c in range(n):
            states.append([states[c][i] * dec[c][:, h] + upd[c][i] for i, h in enumerate(heads)])
        for i in range(HG_HEADS):
            state_ref[i] = states[n][i]
        for c in range(n):
            for i, h in enumerate(heads):
                finish(o_intra[c][i] + _dot_nt(qe[c][:, h], states[c][i].astype(MXU_DT)), chunks[c], h)

    lf = logf_s[...]
    total = jnp.min(jnp.concatenate(
        [jnp.sum(lf[c * C:(c + 1) * C], axis=0, keepdims=True) for c in range(ts // C)], axis=0))
    lax.cond(total >= HG_FAST_LIMIT, fast, robust)


def _hgrn(proj_h, loglb, log1mlb, onemlb, gw, amat, masks, B, S, ts=512):
    nt = S // ts
    sec = lambda s: (lambda b, t: (b * nt + t, s))
    vec = pl.BlockSpec((1, HG_WIDTH), lambda b, t: (0, 0))
    return pl.pallas_call(
        _hgrn_kernel,
        grid=(B, nt),
        in_specs=[pl.BlockSpec((ts, HG_WIDTH), sec(0)),
                  pl.BlockSpec((ts, HG_WIDTH), sec(1)),
                  pl.BlockSpec((ts, HG_WIDTH), sec(2)),
                  pl.BlockSpec((ts, HG_WIDTH), sec(3)),
                  vec, vec, vec,
                  pl.BlockSpec((1, HG_HEAD_DIM), lambda b, t: (0, 0)),
                  pl.BlockSpec(amat.shape, lambda b, t: (0, 0)),
                  pl.BlockSpec(masks.shape, lambda b, t: (0, 0, 0))],
        out_specs=pl.BlockSpec((ts, HG_WIDTH), lambda b, t: (b * nt + t, 0)),
        out_shape=jax.ShapeDtypeStruct((B * S, HG_WIDTH), MXU_DT),
        scratch_shapes=[pltpu.VMEM((HG_HEADS, HG_HEAD_DIM, HG_HEAD_DIM), F32),
                        pltpu.VMEM((ts, HG_WIDTH), F32),
                        pltpu.VMEM((ts, HG_WIDTH), F32)],
        compiler_params=_cparams(("arbitrary", "arbitrary")),
        name="hgrn2",
    )(proj_h, proj_h, proj_h, proj_h, loglb, log1mlb, onemlb, gw, amat, masks)


def _gelu_tanh(x):
    return 0.5 * x * (1.0 + jnp.tanh(math.sqrt(2.0 / math.pi) * (x + 0.044715 * (x * x * x))))


def _compress_kernel(kc_ref, vc_ref, pka_ref, pkb_ref, wka_ref, wkb_ref, wk2_ref,
                     pva_ref, pvb_ref, wva_ref, wvb_ref, wv2_ref, ok_ref, ov_ref):
    def one(x_ref, pa_ref, pb_ref, wa_ref, wb_ref, w2_ref):
        x = x_ref[...]
        n = x.shape[0]
        ha = _dot(x, wa_ref[...])
        hb = _dot(x, wb_ref[...])
        posw = _dot(pa_ref[...], wa_ref[...]) + _dot(pb_ref[...], wb_ref[...])
        h = ha + pltpu.roll(hb, n - 1, axis=0) + posw[0:1]
        return _dot(_gelu_tanh(h).astype(MXU_DT), w2_ref[...])

    ok_ref[...] = one(kc_ref, pka_ref, pkb_ref, wka_ref, wkb_ref, wk2_ref).astype(ok_ref.dtype)
    ov_ref[...] = one(vc_ref, pva_ref, pvb_ref, wva_ref, wvb_ref, wv2_ref).T.astype(ov_ref.dtype)


def _compress(kc3, vc3, kparams, vparams):
    B, n, w = kc3.shape
    x_spec = pl.BlockSpec((None, n, w), lambda b: (b, 0, 0))
    full = lambda a: pl.BlockSpec(a.shape, lambda b: (0,) * a.ndim)
    return pl.pallas_call(
        _compress_kernel,
        grid=(B,),
        in_specs=[x_spec, x_spec] + [full(a) for a in kparams] + [full(a) for a in vparams],
        out_specs=[pl.BlockSpec((None, n, KV_W), lambda b: (b, 0, 0)),
                   pl.BlockSpec((None, KV_W, n), lambda b: (b, 0, 0))],
        out_shape=[jax.ShapeDtypeStruct((B, n, KV_W), MXU_DT),
                   jax.ShapeDtypeStruct((B, KV_W, n), MXU_DT)],
        compiler_params=_cparams(("parallel",)),
        name="nsa_compress",
    )(kc3, vc3, *kparams, *vparams)


def _compress_params(pos, w1, w2):
    G, Dh, Hd = NSA_KV_GROUPS, NSA_HEAD_DIM, CMP_HIDDEN
    eye = jnp.eye(G, dtype=F32)

    def half(wh, ph):
        wbig = jnp.einsum('rdj,gk->rgdkj', wh.reshape(CMP_STRIDE, Dh, Hd), eye).reshape(CMP_STRIDE * G * Dh, G * Hd)
        pbig = jnp.broadcast_to(ph[:, None, :], (CMP_STRIDE, G, Dh)).reshape(1, CMP_STRIDE * G * Dh)
        return jnp.broadcast_to(pbig, (8, pbig.shape[1])).astype(MXU_DT), wbig.astype(MXU_DT)

    pa, wa = half(w1[:CMP_STRIDE * Dh], pos[:CMP_STRIDE])
    pb, wb = half(w1[CMP_STRIDE * Dh:], pos[CMP_STRIDE:])
    w2big = jnp.einsum('jd,gk->gjkd', w2, eye).reshape(G * Hd, G * Dh).astype(MXU_DT)
    return pa, pb, wa, wb, w2big


def _t5_bucket_np(n):
    n = np.maximum(n, 0)
    max_exact = N_BUCKETS // 2
    nf = np.maximum(n, 1).astype(np.float32)
    large = max_exact + (np.log(nf / np.float32(max_exact)) / np.float32(math.log(MAX_DISTANCE / max_exact))
                         * np.float32(N_BUCKETS - max_exact)).astype(np.int32)
    large = np.minimum(large, N_BUCKETS - 1)
    return np.where(n < max_exact, n, large).astype(np.int32)


def _nsa_tables(rel_bias):
    rb = rel_bias.astype(F32)

    def per_distance(dist, valid, rel_to_far):
        b = rb[jnp.asarray(_t5_bucket_np(dist))]
        if rel_to_far:
            b = b - rb[N_BUCKETS - 1]
        return (jnp.where(jnp.asarray(valid)[:, None], b, MASK_NEG) * LOG2E).T

    def skewed(dist_of, n_rows, n_cols, step, valid, rel_to_far):
        neg = step * (n_rows - 1)
        period = n_cols + neg + step
        k = np.arange(period)
        k = np.where(k < n_cols + step, k, k - period)
        d = dist_of(k)
        f = per_distance(d, valid(d), rel_to_far)
        flat = jnp.tile(f, (1, n_rows))[:, :n_rows * (period - step)]
        return flat.reshape(NSA_HEADS, n_rows, period - step)[:, :, :n_cols]

    def key_major(n_keys, first_dist, valid, rel_to_far):
        m = skewed(lambda k: first_dist - k, TQ, n_keys, 1, valid, rel_to_far)
        return jnp.transpose(m, (2, 0, 1)).reshape(n_keys, COLS)

    t_sel = key_major(DIAG_W, TQ, lambda d: d >= 0, True)
    t_win = key_major(WIN_W, WINDOW, lambda d: (d >= 0) & (d < WINDOW), False)
    m = skewed(lambda k: k + CMP_STRIDE * 16 - (CMP_BLOCK - 1), CMP_TAB, TQ, CMP_STRIDE, lambda d: d >= 0, True)
    t_cmp = jnp.transpose(m, (1, 0, 2)).reshape(CMP_TAB, COLS)
    return t_sel, t_win, t_cmp


def _nsa_consts(S):
    n_slc = S // SLC_BLOCK
    n_cmp_pad = S // CMP_STRIDE
    assert n_slc < LANES and n_cmp_pad >= CMP_BAND
    far = _t5_bucket_np(np.arange(8 * MAX_DISTANCE)) == N_BUCKETS - 1
    far_dist = int(np.max(np.nonzero(~far)[0])) + 1
    assert TQ + 1 >= far_dist and 17 * CMP_STRIDE - (CMP_BLOCK - 1) >= far_dist
    eind = np.zeros((S, LANES), np.float32)
    eind[np.arange(S), np.arange(S) // SLC_BLOCK] = 1.0
    cs = np.arange(n_cmp_pad)[None, :] * CMP_STRIDE
    ss = np.arange(LANES)[:, None] * SLC_BLOCK
    ov = np.clip(np.minimum(cs + CMP_BLOCK, ss + SLC_BLOCK) - np.maximum(cs, ss), 0, None)
    agg_t = (ov / CMP_STRIDE).astype(np.float32)
    agg_t[:, n_cmp_pad - 1] = 0.0
    agg_t[n_slc:] = 0.0
    assert n_cmp_pad <= SUBLANES * LANES
    cind = np.zeros((n_cmp_pad, LANES), np.float32)
    cind[np.arange(n_cmp_pad), np.arange(n_cmp_pad) // SUBLANES] = 1.0
    return eind, agg_t, cind


def _nsa_kernel(q_ref, gt_ref, kc_ref, vct_ref, ks_ref, vs_ref, kw_ref, vw_ref,
                tsel_ref, twin_ref, tcmp_ref, eind_ref, aggt_ref, cind_ref, tmax_ref,
                o_ref, kaug_s, vselt_s, kwin_s, vwint_s, kcaug_s, sc_s, m_s, acc_s, osel_s, owin_s, knorm_s):
    S = ks_ref.shape[0]
    n_slc = S // SLC_BLOCK
    n_cmp_pad = kc_ref.shape[0]
    qt = pl.program_id(1)
    t0 = pl.multiple_of(qt * TQ, TQ)

    @pl.when(qt == 0)
    def _():
        lane1 = lax.broadcasted_iota(jnp.int32, (1, LANES), 1)
        kaug_s[0:TQ, 0:LANES] = jnp.zeros((TQ, LANES), MXU_DT)
        kaug_s[0:TQ, LANES:] = jnp.broadcast_to(jnp.where(lane1 == LANES - 1, 1.0, 0.0).astype(MXU_DT), (TQ, LANES))
        kaug_s[TQ:, 0:LANES] = ks_ref[...]
        kaug_s[TQ:, LANES:] = eind_ref[...]
        kcaug_s[:, 0:LANES] = kc_ref[...]
        kcaug_s[:, LANES:] = cind_ref[...]
        rg = lax.broadcasted_iota(jnp.int32, (LANES, LANES), 0) // NSA_HEAD_DIM
        cg = lax.broadcasted_iota(jnp.int32, (LANES, LANES), 1) // NSA_HEAD_DIM
        same_group = jnp.where(rg == cg, 1.0, 0.0).astype(MXU_DT)
        for i, ref in enumerate((ks_ref, kw_ref)):
            n2 = _dot(jnp.square(ref[...].astype(F32)).astype(MXU_DT), same_group)
            n2 = jnp.max(n2, axis=0, keepdims=True)
            for g in range(NSA_KV_GROUPS):
                knorm_s[i * NSA_KV_GROUPS + g] = jnp.sqrt(jnp.max(n2[:, g * NSA_HEAD_DIM:(g + 1) * NSA_HEAD_DIM]))
        ones_rows = (lax.broadcasted_iota(jnp.int32, (VROWS - LANES, 1), 0) == 0).astype(MXU_DT)
        vselt_s[0:LANES, 0:TQ] = jnp.zeros((LANES, TQ), MXU_DT)
        vselt_s[0:LANES, TQ:] = vs_ref[...].astype(F32).T.astype(MXU_DT)
        vselt_s[LANES:, :] = jnp.broadcast_to(ones_rows, (VROWS - LANES, TQ + S))
        kwin_s[0:WINDOW, 0:LANES] = jnp.zeros((WINDOW, LANES), MXU_DT)
        kwin_s[0:WINDOW, LANES:] = jnp.broadcast_to(jnp.where(lane1 == 0, 1.0, 0.0).astype(MXU_DT), (WINDOW, LANES))
        kwin_s[WINDOW:, 0:LANES] = kw_ref[...]
        kwin_s[WINDOW:, LANES:] = jnp.zeros((S, LANES), MXU_DT)
        vwint_s[0:LANES, 0:WINDOW] = jnp.zeros((LANES, WINDOW), MXU_DT)
        vwint_s[0:LANES, WINDOW:] = vw_ref[...].astype(F32).T.astype(MXU_DT)
        vwint_s[LANES:, :] = jnp.broadcast_to(ones_rows, (VROWS - LANES, WINDOW + S))

    lane = lax.broadcasted_iota(jnp.int32, (1, LANES), 1)
    lo_lane = lane < NSA_HEAD_DIM
    zero = jnp.zeros((), MXU_DT)

    qg = jnp.concatenate([jnp.where(lo_lane if g == 0 else jnp.logical_not(lo_lane),
                                    q_ref[:, j * LANES:(j + 1) * LANES], zero)
                          for g in range(NSA_KV_GROUPS) for j in range(NSA_HPG)], axis=0)

    n0 = pl.multiple_of(qt * (TQ // CMP_STRIDE), SUBLANES)
    late = jnp.where(lane > n0 // SUBLANES, MASK_NEG, 0.0).astype(MXU_DT)
    sc_s[...] = _dot_nt(kcaug_s[...], jnp.concatenate([qg, jnp.broadcast_to(late, (COLS, LANES))], axis=1))
    band = pl.multiple_of(jnp.maximum(n0 - 16, 0), SUBLANES)
    toff = pl.multiple_of(16 + band - n0, SUBLANES)
    sc_s[pl.ds(band, CMP_BAND), :] += tcmp_ref[pl.ds(toff, CMP_BAND), :]
    s_c = sc_s[...]
    m_c = jnp.max(s_c, axis=0, keepdims=True)
    p_c = jnp.exp2(s_c - m_c)
    l_c = jnp.sum(p_c, axis=0, keepdims=True)
    p_c = p_c * jnp.where(m_c > 0.5 * MASK_NEG, 1.0 / l_c, 0.0)
    o_c = _dot(vct_ref[...], p_c.astype(MXU_DT))

    p_sum = jnp.concatenate(
        [sum(p_c[:, (g * NSA_HPG + j) * TQ:(g * NSA_HPG + j + 1) * TQ] for j in range(NSA_HPG))
         for g in range(NSA_KV_GROUPS)], axis=1)
    n_rows = -(-n_slc // SUBLANES) * SUBLANES
    imp = _dot_split(p_sum, aggt_ref[...], 3, x_is_rhs=True)[0:n_rows]
    blk = lax.broadcasted_iota(jnp.int32, (n_rows, 1), 0)
    blk_f = blk.astype(F32)
    tq_pos = t0 + lax.broadcasted_iota(jnp.int32, (1, TQ), 1)
    tpos = jnp.concatenate([tq_pos] * NSA_KV_GROUPS, axis=1)
    diff = tpos // SLC_BLOCK - blk
    forced = (blk == 0) | ((diff >= 0) & (diff < SLC_LOCAL))
    score = jnp.where(forced, -jnp.inf, jnp.where(blk * SLC_BLOCK <= tpos, imp, -BIG))
    score = jnp.where(blk < n_slc, score, -jnp.inf)
    sel = jnp.where(forced, 1.0, 0.0)
    for _ in range(max(min(SLC_TOPK, n_slc) - (1 + SLC_LOCAL), 0)):
        mx = jnp.max(score, axis=0, keepdims=True)
        idx = jnp.min(jnp.where(score == mx, blk_f, float(LANES)), axis=0, keepdims=True)
        hit = blk_f == idx
        sel = jnp.where(hit, 1.0, sel)
        score = jnp.where(hit, -jnp.inf, score)
    if n_rows < LANES:
        sel = jnp.concatenate([sel, jnp.zeros((LANES - n_rows, sel.shape[1]), F32)], axis=0)
    selb = (sel - 1.0) * (-SEL_NEG)
    selb_t = [selb[:, g * TQ:(g + 1) * TQ].T for g in range(NSA_KV_GROUPS)]
    selb_rows = jnp.concatenate([selb_t[g] for g in range(NSA_KV_GROUPS) for _ in range(NSA_HPG)], axis=0)

    qn = jnp.sqrt(_dot_nt(jnp.ones((SUBLANES, LANES), MXU_DT), jnp.square(qg.astype(F32)).astype(MXU_DT))[0:1])
    first_group = lax.broadcasted_iota(jnp.int32, (1, COLS), 1) < COLS // NSA_KV_GROUPS
    b_sel = qn * jnp.where(first_group, knorm_s[0], knorm_s[1]) * BOUND_SLACK + (tmax_ref[0] + 1.0)
    b_win = qn * jnp.where(first_group, knorm_s[2], knorm_s[3]) * BOUND_SLACK + (tmax_ref[1] + 1.0)
    bounded = jnp.maximum(jnp.max(b_sel), jnp.max(b_win)) <= BOUND_LIMIT
    qa = jnp.concatenate([qg, selb_rows.astype(MXU_DT)], axis=1)
    qw = jnp.concatenate(
        [qg, jnp.broadcast_to(jnp.where(lane == 0, SEL_NEG, 0.0).astype(MXU_DT), (COLS, LANES))], axis=1)

    def attend(use_bound):
        if not use_bound:
            m_s[...] = jnp.full(m_s.shape, MASK_NEG, F32)
        acc_s[...] = jnp.zeros(acc_s.shape, F32)

        def update(start, width, table_ref=None):
            s = _dot_nt(kaug_s[pl.ds(start, width), :], qa)
            if table_ref is not None:
                s = s + table_ref[...]
            vt = vselt_s[:, pl.ds(start, width)]
            if use_bound:
                acc_s[...] += _dot(vt, jnp.exp2((s - b_sel).astype(MXU_DT)))
            else:
                m_prev = m_s[...]
                m_new = jnp.maximum(m_prev, jnp.max(s, axis=0, keepdims=True))
                p = jnp.exp2((s - m_new).astype(MXU_DT))
                acc_s[...] = jnp.exp2(m_prev - m_new) * acc_s[...] + _dot(vt, p)
                m_s[...] = m_new

        def far_tile(i, carry):
            update(pl.multiple_of(TQ + i * SEL_TK, LANES), SEL_TK)
            return carry

        n_far = jnp.maximum(t0 - TQ, 0)
        n_full = n_far // SEL_TK
        lax.fori_loop(0, n_full, far_tile, 0)
        pos = n_full * SEL_TK
        width = SEL_TK // 2
        while width >= LANES:
            digit = ((n_far - n_full * SEL_TK) & width) != 0
            pl.when(digit)(functools.partial(update, pl.multiple_of(TQ + pos, LANES), width))
            pos = pos + jnp.where(digit, width, 0)
            width //= 2
        update(t0, DIAG_W, tsel_ref)
        acc = acc_s[...]
        osel_s[...] = acc[0:LANES] * (1.0 / acc[LANES:LANES + 1])

        s_w = _dot_nt(kwin_s[pl.ds(t0, WIN_W), :], qw) + twin_ref[...]
        s_w = s_w - (b_win if use_bound else jnp.max(s_w, axis=0, keepdims=True))
        acc = _dot(vwint_s[:, pl.ds(t0, WIN_W)], jnp.exp2(s_w.astype(MXU_DT)))
        owin_s[...] = acc[0:LANES] * (1.0 / acc[LANES:LANES + 1])

    lax.cond(bounded, lambda: attend(True), lambda: attend(False))
    o_s = osel_s[...]
    o_w = owin_s[...]

    gsig_t = (1.0 / (1.0 + jnp.exp(-gt_ref[...]))).T
    row_lo = lax.broadcasted_iota(jnp.int32, (LANES, 1), 0) < NSA_HEAD_DIM
    for j in range(NSA_HPG):
        c0, c1 = j * TQ, (NSA_HPG + j) * TQ
        acc = None
        for br, o in enumerate((o_c, o_s, o_w)):
            r0, r1 = br * NSA_HEADS + j, br * NSA_HEADS + NSA_HPG + j
            gate = jnp.where(row_lo, gsig_t[r0:r0 + 1, :], gsig_t[r1:r1 + 1, :])
            term = gate * jnp.where(row_lo, o[:, c0:c0 + TQ], o[:, c1:c1 + TQ])
            acc = term if acc is None else acc + term
        o_ref[:, j * LANES:(j + 1) * LANES] = acc.T.astype(o_ref.dtype)


def _nsa(proj_n, gates, kcmp, vcmp_t, tsel, twin, tcmp, eind, agg_t, cind, B, S):
    nq = S // TQ
    n_cmp_pad = S // CMP_STRIDE
    qcols = NSA_WIDTH // LANES
    tok = lambda b, t: (b * nq + t, 0)
    kv = lambda c: pl.BlockSpec((S, KV_W), lambda b, t: (b, qcols + c))
    const = lambda a: pl.BlockSpec(a.shape, lambda b, t: (0,) * a.ndim)
    return pl.pallas_call(
        _nsa_kernel,
        grid=(B, nq),
        in_specs=[pl.BlockSpec((TQ, NSA_WIDTH), tok),
                  pl.BlockSpec((TQ, LANES), tok),
                  pl.BlockSpec((None, n_cmp_pad, KV_W), lambda b, t: (b, 0, 0)),
                  pl.BlockSpec((None, KV_W, n_cmp_pad), lambda b, t: (b, 0, 0)),
                  kv(0), kv(1), kv(2), kv(3),
                  const(tsel), const(twin), const(tcmp), const(eind), const(agg_t), const(cind),
                  pl.BlockSpec(memory_space=pltpu.SMEM)],
        out_specs=pl.BlockSpec((TQ, NSA_WIDTH), tok),
        out_shape=jax.ShapeDtypeStruct((B * S, NSA_WIDTH), MXU_DT),
        scratch_shapes=[pltpu.VMEM((TQ + S, 2 * LANES), MXU_DT),
                        pltpu.VMEM((VROWS, TQ + S), MXU_DT),
                        pltpu.VMEM((WINDOW + S, 2 * LANES), MXU_DT),
                        pltpu.VMEM((VROWS, WINDOW + S), MXU_DT),
                        pltpu.VMEM((n_cmp_pad, 2 * LANES), MXU_DT),
                        pltpu.VMEM((n_cmp_pad, COLS), F32),
                        pltpu.VMEM((1, COLS), F32),
                        pltpu.VMEM((VROWS, COLS), F32),
                        pltpu.VMEM((LANES, COLS), F32),
                        pltpu.VMEM((LANES, COLS), F32),
                        pltpu.SMEM((2 * NSA_KV_GROUPS,), F32)],
        compiler_params=_cparams(("arbitrary", "arbitrary")),
        name="nsa_attention",
    )(proj_n, gates, kcmp, vcmp_t, proj_n, proj_n, proj_n, proj_n, tsel, twin, tcmp, eind, agg_t, cind,
      jnp.stack([jnp.max(tsel), jnp.max(twin)]))


POST_SPLIT = 1
POST_TF = 256


def _layer_norm(x, g, b):
    mu = jnp.mean(x, axis=-1, keepdims=True)
    xc = x - mu
    var = jnp.mean(xc * xc, axis=-1, keepdims=True)
    return xc * lax.rsqrt(var + 1e-5) * g + b


def _post_kernel(x_ref, oa_ref, ob_ref, woa_ref, wob_ref, g1_ref, b1_ref, wg_ref, wu_ref, wd_ref, g2_ref, b2_ref,
                 o_ref, acc_ref):
    tm = x_ref.shape[0]
    halves = [slice(r, r + tm // POST_SPLIT) for r in range(0, tm, tm // POST_SPLIT)]
    x1 = []
    for r in halves:
        mix = _dot(oa_ref[r, :], woa_ref[...]) + _dot(ob_ref[r, :], wob_ref[...])
        x1.append(_layer_norm(ALPHA * x_ref[r, :] + mix, g1_ref[...], b1_ref[...]))
    xb = [v.astype(MXU_DT) for v in x1]
    tf = POST_TF
    for c in range(0, D_FF, tf):
        for r, xh in zip(halves, xb):
            gate = _dot(xh, wg_ref[:, c:c + tf])
            up = _dot(xh, wu_ref[:, c:c + tf])
            h = (gate / (1.0 + jnp.exp(-gate)) * up).astype(MXU_DT)
            d = _dot(h, wd_ref[c:c + tf, :])
            if c == 0:
                acc_ref[r, :] = d
            else:
                acc_ref[r, :] += d
    for r, v in zip(halves, x1):
        o_ref[r, :] = _layer_norm(ALPHA * v + acc_ref[r, :], g2_ref[...], b2_ref[...])


def _post(x2, oa, ob, woa, wob, g1, b1, wg, wu, wd, g2, b2, layer, tm=512):
    T = x2.shape[0]
    row = lambda i: (i, 0)
    const = lambda a: pl.BlockSpec((None,) + a.shape[1:], lambda i: (layer, 0, 0), pipeline_mode=pl.Buffered(1))
    return pl.pallas_call(
        _post_kernel,
        grid=(T // tm,),
        in_specs=[pl.BlockSpec((tm, D_MODEL), row),
                  pl.BlockSpec((tm, HG_WIDTH), row),
                  pl.BlockSpec((tm, NSA_WIDTH), row),
                  const(woa), const(wob), const(g1), const(b1),
                  const(wg), const(wu), const(wd), const(g2), const(b2)],
        out_specs=pl.BlockSpec((tm, D_MODEL), row),
        out_shape=jax.ShapeDtypeStruct((T, D_MODEL), F32),
        scratch_shapes=[pltpu.VMEM((tm, D_MODEL), F32)],
        compiler_params=_cparams(("parallel",)),
        name="outproj_ffn",
    )(x2, oa, ob, woa, wob, g1, b1, wg, wu, wd, g2, b2)


def _split_w_in(w):
    depth = w.shape[0]
    o = 4 * HG_WIDTH
    wh = w[:, :, :o]
    wq = w[:, :, o:o + NSA_WIDTH].reshape(depth, D_MODEL, NSA_KV_GROUPS, NSA_HPG, NSA_HEAD_DIM)
    wq = jnp.swapaxes(wq, 2, 3).reshape(depth, D_MODEL, NSA_WIDTH)
    wq = wq * (NSA_HEAD_DIM ** -0.5 * LOG2E)
    o += NSA_WIDTH
    kc, vc, ks, vs, kw, vw = (w[:, :, o + i * KV_W:o + (i + 1) * KV_W] for i in range(6))
    o += 6 * KV_W
    wgt = jnp.pad(w[:, :, o:], ((0, 0), (0, 0), (0, LANES - 3 * NSA_HEADS)))
    wn = jnp.concatenate([wq, ks, vs, kw, vw], axis=2)
    wc = jnp.concatenate([kc, vc], axis=2)
    return wh.astype(MXU_DT), wn.astype(MXU_DT), wc.astype(MXU_DT), wgt.astype(MXU_DT)


def kernel(x, w_in, hg_lb_param, hg_norm_w, cmp_pos_k, cmp_w1_k, cmp_w2_k, cmp_pos_v, cmp_w1_v, cmp_w2_v,
           rel_bias, w_out, ln1_g, ln1_b, w_ffn_gate, w_ffn_up, w_ffn_down, ln2_g, ln2_b):
    B, S, _ = x.shape
    T = B * S
    depth = w_in.shape[0]

    p = jax.nn.softmax(hg_lb_param.astype(F32), axis=0)
    c = jnp.cumsum(p, axis=0)
    lbs = c - c[0:1]
    loglb, log1mlb, onemlb = jnp.log(lbs), jnp.log1p(-lbs), 1.0 - lbs

    amat_np, masks_np = _hgrn_consts()
    amat, masks = jnp.asarray(amat_np, MXU_DT), jnp.asarray(masks_np)
    eind, agg_t, cind = (jnp.asarray(a, MXU_DT) for a in _nsa_consts(S))
    tsel, twin, tcmp = _nsa_tables(rel_bias)

    wh, wn, wc, wgt = _split_w_in(w_in)
    woa = w_out[:, :HG_WIDTH].astype(MXU_DT)
    wob = w_out[:, HG_WIDTH:].reshape(depth, NSA_KV_GROUPS, NSA_HPG, NSA_HEAD_DIM, D_MODEL)
    wob = jnp.swapaxes(wob, 1, 2).reshape(depth, NSA_WIDTH, D_MODEL).astype(MXU_DT)
    wfg, wfu, wfd = w_ffn_gate.astype(MXU_DT), w_ffn_up.astype(MXU_DT), w_ffn_down.astype(MXU_DT)
    ln = [a[:, None, :] for a in (ln1_g, ln1_b, ln2_g, ln2_b)]

    x2 = x.reshape(T, D_MODEL).astype(F32)
    for l in range(depth):
        proj_h, proj_n, kc, vc, gates = _inproj(x2, wh, wn, wc, wgt, l)
        o_a = _hgrn(proj_h, loglb[l:l + 1], log1mlb[l:l + 1], onemlb[l:l + 1], hg_norm_w[l][None, :], amat, masks, B, S)
        shape3 = (B, S // CMP_STRIDE, CMP_STRIDE * KV_W)
        kcmp, vcmp_t = _compress(kc.reshape(shape3), vc.reshape(shape3),
                                 _compress_params(cmp_pos_k[l], cmp_w1_k[l], cmp_w2_k[l]),
                                 _compress_params(cmp_pos_v[l], cmp_w1_v[l], cmp_w2_v[l]))
        o_b = _nsa(proj_n, gates, kcmp, vcmp_t, tsel, twin, tcmp, eind, agg_t, cind, B, S)
        x2 = _post(x2, o_a, o_b, woa, wob, ln[0], ln[1], wfg, wfu, wfd, ln[2], ln[3], l)
    return x2.reshape(B, S, D_MODEL).astype(x.dtype)
```

```python
import functools
import math

import numpy as np
import jax
import jax.numpy as jnp
from jax import lax
from jax.experimental import pallas as pl
from jax.experimental.pallas import tpu as pltpu

F32 = jnp.float32
MXU_DT = jnp.bfloat16

D_MODEL = 1024
DEPTH = 4
HG_WIDTH = 512
HG_HEAD_DIM = 128
HG_HEADS = 4
HG_CHUNK = 64
NSA_WIDTH = 512
NSA_HEAD_DIM = 64
NSA_HEADS = 8
NSA_KV_GROUPS = 2
NSA_HPG = 4
CMP_BLOCK = 32
CMP_STRIDE = 16
CMP_HIDDEN = 128
SLC_BLOCK = 64
SLC_TOPK = 16
SLC_LOCAL = 2
WINDOW = 512
N_BUCKETS = 32
MAX_DISTANCE = 128
D_FF = 2816
ALPHA = (2.0 * DEPTH) ** 0.25
KV_W = NSA_KV_GROUPS * NSA_HEAD_DIM
LANES = 128
SUBLANES = 8
BIG = 1e9
MASK_NEG = -1e30
SEL_NEG = MASK_NEG
LOG2E = math.log2(math.e)

VMEM_LIMIT = 56 * 1024 * 1024

TQ = 128
COLS = NSA_HEADS * TQ
SEL_TK = 1024
BOUND_LIMIT = 40.0
BOUND_SLACK = 1.01
VROWS = LANES + 16
DIAG_W = 2 * TQ
WIN_W = WINDOW + TQ
CMP_BAND = 24
CMP_TAB = 40
HEAD_PERM = tuple(g * NSA_HPG + j for j in range(NSA_HPG) for g in range(NSA_KV_GROUPS))


def _dot(a, b):
    return jnp.dot(a, b, preferred_element_type=F32)


def _dot_nt(a, b):
    return lax.dot_general(a, b, (((1,), (1,)), ((), ())), preferred_element_type=F32)


def _dot_tn(a, b):
    return lax.dot_general(a, b, (((0,), (0,)), ((), ())), preferred_element_type=F32)


def _dot_split(x, w, parts, x_is_rhs=False):
    acc = None
    r = x
    for p in range(parts):
        h = r.astype(MXU_DT)
        d = _dot(w, h) if x_is_rhs else _dot(h, w)
        acc = d if acc is None else acc + d
        if p + 1 < parts:
            r = r - h.astype(F32)
    return acc


def _cparams(sem):
    return pltpu.CompilerParams(dimension_semantics=sem, vmem_limit_bytes=VMEM_LIMIT)


def _inproj_kernel(x_ref, wh_ref, wn_ref, wc_ref, wg_ref, oh_ref, on_ref, okc_ref, ovc_ref, og_ref):
    xb = x_ref[...].astype(MXU_DT)
    nh = oh_ref.shape[1]
    for c in range(0, nh, 512):
        oh_ref[:, c:c + 512] = _dot(xb, wh_ref[:, c:c + 512])
    nn = on_ref.shape[1]
    for c in range(0, nn, 512):
        w = min(512, nn - c)
        on_ref[:, c:c + w] = _dot(xb, wn_ref[:, c:c + w]).astype(on_ref.dtype)
    kv = _dot(xb, wc_ref[...])
    okc_ref[...] = kv[:, :KV_W].astype(okc_ref.dtype)
    ovc_ref[...] = kv[:, KV_W:].astype(ovc_ref.dtype)
    og_ref[...] = _dot(xb, wg_ref[...])


def _inproj(x2, wh, wn, wc, wg, layer, tm=512):
    T = x2.shape[0]
    nh, nn = wh.shape[2], wn.shape[2]
    full = lambda i: (layer, 0, 0)
    row = lambda i: (i, 0)
    return pl.pallas_call(
        _inproj_kernel,
        grid=(T // tm,),
        in_specs=[pl.BlockSpec((tm, D_MODEL), row),
                  pl.BlockSpec((None, D_MODEL, nh), full),
                  pl.BlockSpec((None, D_MODEL, nn), full),
                  pl.BlockSpec((None, D_MODEL, 2 * KV_W), full),
                  pl.BlockSpec((None, D_MODEL, LANES), full)],
        out_specs=[pl.BlockSpec((tm, nh), row),
                   pl.BlockSpec((tm, nn), row),
                   pl.BlockSpec((tm, KV_W), row),
                   pl.BlockSpec((tm, KV_W), row),
                   pl.BlockSpec((tm, LANES), row)],
        out_shape=[jax.ShapeDtypeStruct((T, nh), F32),
                   jax.ShapeDtypeStruct((T, nn), MXU_DT),
                   jax.ShapeDtypeStruct((T, KV_W), MXU_DT),
                   jax.ShapeDtypeStruct((T, KV_W), MXU_DT),
                   jax.ShapeDtypeStruct((T, LANES), F32)],
        compiler_params=_cparams(("parallel",)),
        name="inproj",
    )(x2, wh, wn, wc, wg)


_HG_LEVELS = (32, 16, 8, 4, 2, 1)
_HG_NLEV = len(_HG_LEVELS)
HG_FAST_LIMIT = -80.0
HG_PREP_ROWS = 16


def _hgrn_consts():
    C = HG_CHUNK
    a = np.zeros((8 * C, C), np.float32)
    m = np.zeros((_HG_NLEV + 1, C, C), np.float32)
    for l, h in enumerate(_HG_LEVELS):
        for r in range(C):
            mid = (r // (2 * h)) * 2 * h + h
            if r >= mid:
                a[l * C + r, mid:r + 1] = 1.0
                m[l, r, mid - h:mid] = 1.0
            else:
                a[l * C + r, r + 1:mid] = -1.0
    for r in range(C):
        a[6 * C + r, :r + 1] = 1.0
        a[7 * C + r, r + 1:] = 1.0
        m[_HG_NLEV, r, r] = 1.0
    return a, m


def _hgrn_kernel(q_ref, f_ref, i_ref, g_ref, loglb_ref, log1mlb_ref, onemlb_ref, gw_ref, amat_ref, mask_ref,
                 o_ref, state_ref, logf_s, k_s):
    C, Dk = HG_CHUNK, HG_HEAD_DIM
    ts = q_ref.shape[0]

    @pl.when(pl.program_id(1) == 0)
    def _():
        state_ref[...] = jnp.zeros_like(state_ref)

    a = loglb_ref[...]
    for r in range(0, ts, HG_PREP_ROWS):
        rows = slice(r, r + HG_PREP_ROWS)
        z = f_ref[rows, :]
        e = jnp.exp(-jnp.abs(z))
        e1 = 1.0 + e
        bt = log1mlb_ref[...] + (jnp.minimum(z, 0.0) - jnp.log(e1))
        logf_s[rows, :] = jnp.maximum(a, bt) + jnp.log(1.0 + jnp.exp(-jnp.abs(a - bt)))
        k_s[rows, :] = onemlb_ref[...] * jnp.where(z > 0.0, e, 1.0) / e1

    def finish(o, rows, cols):
        o = o * lax.rsqrt(jnp.mean(o * o, axis=-1, keepdims=True) + 1e-6) * gw_ref[...]
        g = g_ref[rows, cols]
        o_ref[rows, cols] = (o * (g / (1.0 + jnp.exp(-g)))).astype(o_ref.dtype)

    def robust():
        amat = amat_ref[...]

        def chunk(c, carry):
            r0 = pl.multiple_of(c * C, C)
            rows = pl.ds(r0, C)
            for h in range(HG_HEADS):
                cols = slice(h * Dk, (h + 1) * Dk)
                d = _dot_split(logf_s[rows, cols], amat, 2, x_is_rhs=True)
                q = q_ref[rows, cols]
                k = k_s[rows, cols]
                v = i_ref[rows, cols].astype(MXU_DT)
                sc = _dot_nt(q.astype(MXU_DT), k.astype(MXU_DT)) * mask_ref[_HG_NLEV]
                for l in range(_HG_NLEV):
                    el = jnp.exp(-jnp.abs(d[l * C:(l + 1) * C]))
                    sc = sc + _dot_nt((q * el).astype(MXU_DT), (k * el).astype(MXU_DT)) * mask_ref[l]
                b = d[6 * C:7 * C]
                st = state_ref[h]
                o = _dot(sc.astype(MXU_DT), v) + _dot_nt((q * jnp.exp(b)).astype(MXU_DT), st.astype(MXU_DT))
                kdec = (k * jnp.exp(d[7 * C:8 * C])).astype(MXU_DT)
                state_ref[h] = st * jnp.exp(b[C - 1:C]) + _dot_tn(v, kdec)
                finish(o, rows, cols)
            return carry

        lax.fori_loop(0, ts // C, chunk, 0)

    def fast():
        tril_mat = amat_ref[6 * C:7 * C, :]
        causal = lax.broadcasted_iota(jnp.int32, (C, C), 0) >= lax.broadcasted_iota(jnp.int32, (C, C), 1)
        n = ts // C
        chunks = [slice(c * C, (c + 1) * C) for c in range(n)]
        heads = [slice(h * Dk, (h + 1) * Dk) for h in range(HG_HEADS)]
        b = [_dot_split(logf_s[r, :], tril_mat, 2, x_is_rhs=True) for r in chunks]
        e = [jnp.exp(x) for x in b]
        dec = [x[C - 1:C] for x in e]
        qe = [(q_ref[r, :] * x).astype(MXU_DT) for r, x in zip(chunks, e)]
        kt = [k_s[r, :] * jnp.exp(-x) for r, x in zip(chunks, b)]
        ktb = [x.astype(MXU_DT) for x in kt]
        kdec = [(x * d).astype(MXU_DT) for x, d in zip(kt, dec)]
        v = [i_ref[r, :].astype(MXU_DT) for r in chunks]
        o_intra = [[_dot(jnp.where(causal, _dot_nt(qe[c][:, h], ktb[c][:, h]), 0.0).astype(MXU_DT), v[c][:, h])
                    for h in heads] for c in range(n)]
        upd = [[_dot_tn(v[c][:, h], kdec[c][:, h]) for h in heads] for c in range(n)]
        states = [[state_ref[i] for i in range(HG_HEADS)]]
        for c in range(n):
            states.append([states[c][i] * dec[c][:, h] + upd[c][i] for i, h in enumerate(heads)])
        for i in range(HG_HEADS):
            state_ref[i] = states[n][i]
        for c in range(n):
            for i, h in enumerate(heads):
                finish(o_intra[c][i] + _dot_nt(qe[c][:, h], states[c][i].astype(MXU_DT)), chunks[c], h)

    lf = logf_s[...]
    total = jnp.min(jnp.concatenate(
        [jnp.sum(lf[c * C:(c + 1) * C], axis=0, keepdims=True) for c in range(ts // C)], axis=0))
    lax.cond(total >= HG_FAST_LIMIT, fast, robust)


def _hgrn(proj_h, loglb, log1mlb, onemlb, gw, amat, masks, B, S, ts=512):
    nt = S // ts
    sec = lambda s: (lambda b, t: (b * nt + t, s))
    vec = pl.BlockSpec((1, HG_WIDTH), lambda b, t: (0, 0))
    return pl.pallas_call(
        _hgrn_kernel,
        grid=(B, nt),
        in_specs=[pl.BlockSpec((ts, HG_WIDTH), sec(0)),
                  pl.BlockSpec((ts, HG_WIDTH), sec(1)),
                  pl.BlockSpec((ts, HG_WIDTH), sec(2)),
                  pl.BlockSpec((ts, HG_WIDTH), sec(3)),
                  vec, vec, vec,
                  pl.BlockSpec((1, HG_HEAD_DIM), lambda b, t: (0, 0)),
                  pl.BlockSpec(amat.shape, lambda b, t: (0, 0)),
                  pl.BlockSpec(masks.shape, lambda b, t: (0, 0, 0))],
        out_specs=pl.BlockSpec((ts, HG_WIDTH), lambda b, t: (b * nt + t, 0)),
        out_shape=jax.ShapeDtypeStruct((B * S, HG_WIDTH), MXU_DT),
        scratch_shapes=[pltpu.VMEM((HG_HEADS, HG_HEAD_DIM, HG_HEAD_DIM), F32),
                        pltpu.VMEM((ts, HG_WIDTH), F32),
                        pltpu.VMEM((ts, HG_WIDTH), F32)],
        compiler_params=_cparams(("arbitrary", "arbitrary")),
        name="hgrn2",
    )(proj_h, proj_h, proj_h, proj_h, loglb, log1mlb, onemlb, gw, amat, masks)


def _gelu_tanh(x):
    return 0.5 * x * (1.0 + jnp.tanh(math.sqrt(2.0 / math.pi) * (x + 0.044715 * (x * x * x))))


def _compress_kernel(kc_ref, vc_ref, pka_ref, pkb_ref, wka_ref, wkb_ref, wk2_ref,
                     pva_ref, pvb_ref, wva_ref, wvb_ref, wv2_ref, ok_ref, ov_ref):
    def one(x_ref, pa_ref, pb_ref, wa_ref, wb_ref, w2_ref):
        x = x_ref[...]
        n = x.shape[0]
        ha = _dot(x, wa_ref[...])
        hb = _dot(x, wb_ref[...])
        posw = _dot(pa_ref[...], wa_ref[...]) + _dot(pb_ref[...], wb_ref[...])
        h = ha + pltpu.roll(hb, n - 1, axis=0) + posw[0:1]
        return _dot(_gelu_tanh(h).astype(MXU_DT), w2_ref[...])

    ok_ref[...] = one(kc_ref, pka_ref, pkb_ref, wka_ref, wkb_ref, wk2_ref).astype(ok_ref.dtype)
    ov_ref[...] = one(vc_ref, pva_ref, pvb_ref, wva_ref, wvb_ref, wv2_ref).T.astype(ov_ref.dtype)


def _compress(kc3, vc3, kparams, vparams):
    B, n, w = kc3.shape
    x_spec = pl.BlockSpec((None, n, w), lambda b: (b, 0, 0))
    full = lambda a: pl.BlockSpec(a.shape, lambda b: (0,) * a.ndim)
    return pl.pallas_call(
        _compress_kernel,
        grid=(B,),
        in_specs=[x_spec, x_spec] + [full(a) for a in kparams] + [full(a) for a in vparams],
        out_specs=[pl.BlockSpec((None, n, KV_W), lambda b: (b, 0, 0)),
                   pl.BlockSpec((None, KV_W, n), lambda b: (b, 0, 0))],
        out_shape=[jax.ShapeDtypeStruct((B, n, KV_W), MXU_DT),
                   jax.ShapeDtypeStruct((B, KV_W, n), MXU_DT)],
        compiler_params=_cparams(("parallel",)),
        name="nsa_compress",
    )(kc3, vc3, *kparams, *vparams)


def _compress_params(pos, w1, w2):
    G, Dh, Hd = NSA_KV_GROUPS, NSA_HEAD_DIM, CMP_HIDDEN
    eye = jnp.eye(G, dtype=F32)

    def half(wh, ph):
        wbig = jnp.einsum('rdj,gk->rgdkj', wh.reshape(CMP_STRIDE, Dh, Hd), eye).reshape(CMP_STRIDE * G * Dh, G * Hd)
        pbig = jnp.broadcast_to(ph[:, None, :], (CMP_STRIDE, G, Dh)).reshape(1, CMP_STRIDE * G * Dh)
        return jnp.broadcast_to(pbig, (8, pbig.shape[1])).astype(MXU_DT), wbig.astype(MXU_DT)

    pa, wa = half(w1[:CMP_STRIDE * Dh], pos[:CMP_STRIDE])
    pb, wb = half(w1[CMP_STRIDE * Dh:], pos[CMP_STRIDE:])
    w2big = jnp.einsum('jd,gk->gjkd', w2, eye).reshape(G * Hd, G * Dh).astype(MXU_DT)
    return pa, pb, wa, wb, w2big


def _t5_bucket_np(n):
    n = np.maximum(n, 0)
    max_exact = N_BUCKETS // 2
    nf = np.maximum(n, 1).astype(np.float32)
    large = max_exact + (np.log(nf / np.float32(max_exact)) / np.float32(math.log(MAX_DISTANCE / max_exact))
                         * np.float32(N_BUCKETS - max_exact)).astype(np.int32)
    large = np.minimum(large, N_BUCKETS - 1)
    return np.where(n < max_exact, n, large).astype(np.int32)


def _nsa_tables(rel_bias):
    rb = rel_bias.astype(F32)

    def per_distance(dist, valid, rel_to_far):
        b = rb[jnp.asarray(_t5_bucket_np(dist))]
        if rel_to_far:
            b = b - rb[N_BUCKETS - 1]
        return (jnp.where(jnp.asarray(valid)[:, None], b, MASK_NEG) * LOG2E).T

    def skewed(dist_of, n_rows, n_cols, step, valid, rel_to_far):
        neg = step * (n_rows - 1)
        period = n_cols + neg + step
        k = np.arange(period)
        k = np.where(k < n_cols + step, k, k - period)
        d = dist_of(k)
        f = per_distance(d, valid(d), rel_to_far)
        flat = jnp.tile(f, (1, n_rows))[:, :n_rows * (period - step)]
        return flat.reshape(NSA_HEADS, n_rows, period - step)[:, :, :n_cols]

    def key_major(n_keys, first_dist, valid, rel_to_far):
        m = skewed(lambda k: first_dist - k, TQ, n_keys, 1, valid, rel_to_far)
        return jnp.transpose(m, (2, 0, 1)).reshape(n_keys, COLS)

    t_sel = key_major(DIAG_W, TQ, lambda d: d >= 0, True)
    t_win = key_major(WIN_W, WINDOW, lambda d: (d >= 0) & (d < WINDOW), False)
    m = skewed(lambda k: k + CMP_STRIDE * 16 - (CMP_BLOCK - 1), CMP_TAB, TQ, CMP_STRIDE, lambda d: d >= 0, True)
    t_cmp = jnp.transpose(m, (1, 0, 2)).reshape(CMP_TAB, COLS)
    return t_sel, t_win, t_cmp


def _nsa_consts(S):
    n_slc = S // SLC_BLOCK
    n_cmp_pad = S // CMP_STRIDE
    assert n_slc < LANES and n_cmp_pad >= CMP_BAND
    far = _t5_bucket_np(np.arange(8 * MAX_DISTANCE)) == N_BUCKETS - 1
    far_dist = int(np.max(np.nonzero(~far)[0])) + 1
    assert TQ + 1 >= far_dist and 17 * CMP_STRIDE - (CMP_BLOCK - 1) >= far_dist
    eind = np.zeros((S, LANES), np.float32)
    eind[np.arange(S), np.arange(S) // SLC_BLOCK] = 1.0
    cs = np.arange(n_cmp_pad)[None, :] * CMP_STRIDE
    ss = np.arange(LANES)[:, None] * SLC_BLOCK
    ov = np.clip(np.minimum(cs + CMP_BLOCK, ss + SLC_BLOCK) - np.maximum(cs, ss), 0, None)
    agg_t = (ov / CMP_STRIDE).astype(np.float32)
    agg_t[:, n_cmp_pad - 1] = 0.0
    agg_t[n_slc:] = 0.0
    assert n_cmp_pad <= SUBLANES * LANES
    cind = np.zeros((n_cmp_pad, LANES), np.float32)
    cind[np.arange(n_cmp_pad), np.arange(n_cmp_pad) // SUBLANES] = 1.0
    return eind, agg_t, cind


def _nsa_kernel(q_ref, gt_ref, kc_ref, vct_ref, ks_ref, vs_ref, kw_ref, vw_ref,
                tsel_ref, twin_ref, tcmp_ref, eind_ref, aggt_ref, cind_ref, tmax_ref,
                o_ref, kaug_s, vselt_s, kwin_s, vwint_s, kcaug_s, sc_s, m_s, acc_s, osel_s, sw_s, knorm_s):
    S = ks_ref.shape[0]
    n_slc = S // SLC_BLOCK
    n_cmp_pad = kc_ref.shape[0]
    qt = pl.program_id(1)
    t0 = pl.multiple_of(qt * TQ, TQ)

    @pl.when(qt == 0)
    def _():
        lane1 = lax.broadcasted_iota(jnp.int32, (1, LANES), 1)
        kaug_s[0:TQ, 0:LANES] = jnp.zeros((TQ, LANES), MXU_DT)
        kaug_s[0:TQ, LANES:] = jnp.broadcast_to(jnp.where(lane1 == LANES - 1, 1.0, 0.0).astype(MXU_DT), (TQ, LANES))
        kaug_s[TQ:, 0:LANES] = ks_ref[...]
        kaug_s[TQ:, LANES:] = eind_ref[...]
        kcaug_s[:, 0:LANES] = kc_ref[...]
        kcaug_s[:, LANES:] = cind_ref[...]
        rg = lax.broadcasted_iota(jnp.int32, (LANES, LANES), 0) // NSA_HEAD_DIM
        cg = lax.broadcasted_iota(jnp.int32, (LANES, LANES), 1) // NSA_HEAD_DIM
        same_group = jnp.where(rg == cg, 1.0, 0.0).astype(MXU_DT)
        for i, ref in enumerate((ks_ref, kw_ref)):
            n2 = _dot(jnp.square(ref[...].astype(F32)).astype(MXU_DT), same_group)
            n2 = jnp.max(n2, axis=0, keepdims=True)
            for g in range(NSA_KV_GROUPS):
                knorm_s[i * NSA_KV_GROUPS + g] = jnp.sqrt(jnp.max(n2[:, g * NSA_HEAD_DIM:(g + 1) * NSA_HEAD_DIM]))
        ones_rows = (lax.broadcasted_iota(jnp.int32, (VROWS - LANES, 1), 0) == 0).astype(MXU_DT)
        vselt_s[0:LANES, 0:TQ] = jnp.zeros((LANES, TQ), MXU_DT)
        vselt_s[0:LANES, TQ:] = vs_ref[...].astype(F32).T.astype(MXU_DT)
        vselt_s[LANES:, :] = jnp.broadcast_to(ones_rows, (VROWS - LANES, TQ + S))
        kwin_s[0:WINDOW, 0:LANES] = jnp.zeros((WINDOW, LANES), MXU_DT)
        kwin_s[0:WINDOW, LANES:] = jnp.broadcast_to(jnp.where(lane1 == 0, 1.0, 0.0).astype(MXU_DT), (WINDOW, LANES))
        kwin_s[WINDOW:, 0:LANES] = kw_ref[...]
        kwin_s[WINDOW:, LANES:] = jnp.zeros((S, LANES), MXU_DT)
        vwint_s[0:LANES, 0:WINDOW] = jnp.zeros((LANES, WINDOW), MXU_DT)
        vwint_s[0:LANES, WINDOW:] = vw_ref[...].astype(F32).T.astype(MXU_DT)
        vwint_s[LANES:, :] = jnp.broadcast_to(ones_rows, (VROWS - LANES, WINDOW + S))

    lane = lax.broadcasted_iota(jnp.int32, (1, LANES), 1)
    lo_lane = lane < NSA_HEAD_DIM
    zero = jnp.zeros((), MXU_DT)

    qg = jnp.concatenate([jnp.where(lo_lane if g == 0 else jnp.logical_not(lo_lane),
                                    q_ref[:, j * LANES:(j + 1) * LANES], zero)
                          for g in range(NSA_KV_GROUPS) for j in range(NSA_HPG)], axis=0)

    n0 = pl.multiple_of(qt * (TQ // CMP_STRIDE), SUBLANES)
    late = jnp.where(lane > n0 // SUBLANES, MASK_NEG, 0.0).astype(MXU_DT)
    sc_s[...] = _dot_nt(kcaug_s[...], jnp.concatenate([qg, jnp.broadcast_to(late, (COLS, LANES))], axis=1))

    qn = jnp.sqrt(_dot_nt(jnp.ones((SUBLANES, LANES), MXU_DT), jnp.square(qg.astype(F32)).astype(MXU_DT))[0:1])
    first_group = lax.broadcasted_iota(jnp.int32, (1, COLS), 1) < COLS // NSA_KV_GROUPS
    b_sel = qn * jnp.where(first_group, knorm_s[0], knorm_s[1]) * BOUND_SLACK + (tmax_ref[0] + 1.0)
    b_win = qn * jnp.where(first_group, knorm_s[2], knorm_s[3]) * BOUND_SLACK + (tmax_ref[1] + 1.0)
    bounded = jnp.maximum(jnp.max(b_sel), jnp.max(b_win)) <= BOUND_LIMIT

    qw = jnp.concatenate(
        [qg, jnp.broadcast_to(jnp.where(lane == 0, SEL_NEG, 0.0).astype(MXU_DT), (COLS, LANES))], axis=1)
    sw_s[...] = _dot_nt(kwin_s[pl.ds(t0, WIN_W), :], qw) + twin_ref[...]

    band = pl.multiple_of(jnp.maximum(n0 - 16, 0), SUBLANES)
    toff = pl.multiple_of(16 + band - n0, SUBLANES)
    sc_s[pl.ds(band, CMP_BAND), :] += tcmp_ref[pl.ds(toff, CMP_BAND), :]
    s_c = sc_s[...]
    m_c = jnp.max(s_c, axis=0, keepdims=True)
    p_c = jnp.exp2(s_c - m_c)
    l_c = jnp.sum(p_c, axis=0, keepdims=True)
    p_c = p_c * jnp.where(m_c > 0.5 * MASK_NEG, 1.0 / l_c, 0.0)
    o_c = _dot(vct_ref[...], p_c.astype(MXU_DT))

    p_sum = jnp.concatenate(
        [sum(p_c[:, (g * NSA_HPG + j) * TQ:(g * NSA_HPG + j + 1) * TQ] for j in range(NSA_HPG))
         for g in range(NSA_KV_GROUPS)], axis=1)
    n_rows = -(-n_slc // SUBLANES) * SUBLANES
    imp = _dot_split(p_sum, aggt_ref[...], 3, x_is_rhs=True)[0:n_rows]
    blk = lax.broadcasted_iota(jnp.int32, (n_rows, 1), 0)
    blk_f = blk.astype(F32)
    tq_pos = t0 + lax.broadcasted_iota(jnp.int32, (1, TQ), 1)
    tpos = jnp.concatenate([tq_pos] * NSA_KV_GROUPS, axis=1)
    diff = tpos // SLC_BLOCK - blk
    forced = (blk == 0) | ((diff >= 0) & (diff < SLC_LOCAL))
    score = jnp.where(forced, -jnp.inf, jnp.where(blk * SLC_BLOCK <= tpos, imp, -BIG))
    score = jnp.where(blk < n_slc, score, -jnp.inf)
    ref_win = lax.cond(bounded, lambda: b_win, lambda: jnp.max(sw_s[...], axis=0, keepdims=True))
    win_pieces = list(range(0, WIN_W, LANES))
    acc_w = None
    n_rounds = max(min(SLC_TOPK, n_slc) - (1 + SLC_LOCAL), 0)
    for rnd in range(max(n_rounds, len(win_pieces))):
        if rnd < n_rounds:
            mx = jnp.max(score, axis=0, keepdims=True)
            idx = jnp.min(jnp.where(score == mx, blk_f, float(LANES)), axis=0, keepdims=True)
            score = jnp.where(blk_f == idx, -jnp.inf, score)
        if rnd < len(win_pieces):
            r = win_pieces[rnd]
            p_w = jnp.exp2((sw_s[r:r + LANES, :] - ref_win).astype(MXU_DT))
            part = _dot(vwint_s[:, pl.ds(pl.multiple_of(t0 + r, LANES), LANES)], p_w)
            acc_w = part if acc_w is None else acc_w + part
    o_w = acc_w[0:LANES] * (1.0 / acc_w[LANES:LANES + 1])
    sel = jnp.where((score == -jnp.inf) & (blk < n_slc), 1.0, 0.0)
    if n_rows < LANES:
        sel = jnp.concatenate([sel, jnp.zeros((LANES - n_rows, sel.shape[1]), F32)], axis=0)
    selb = (sel - 1.0) * (-SEL_NEG)
    selb_t = [selb[:, g * TQ:(g + 1) * TQ].T for g in range(NSA_KV_GROUPS)]
    selb_rows = jnp.concatenate([selb_t[g] for g in range(NSA_KV_GROUPS) for _ in range(NSA_HPG)], axis=0)

    qa = jnp.concatenate([qg, selb_rows.astype(MXU_DT)], axis=1)

    def attend(use_bound):
        if not use_bound:
            m_s[...] = jnp.full(m_s.shape, MASK_NEG, F32)
        acc_s[...] = jnp.zeros(acc_s.shape, F32)

        def update(start, width, table_ref=None):
            s = _dot_nt(kaug_s[pl.ds(start, width), :], qa)
            if table_ref is not None:
                s = s + table_ref[...]
            vt = vselt_s[:, pl.ds(start, width)]
            if use_bound:
                acc_s[...] += _dot(vt, jnp.exp2((s - b_sel).astype(MXU_DT)))
            else:
                m_prev = m_s[...]
                m_new = jnp.maximum(m_prev, jnp.max(s, axis=0, keepdims=True))
                p = jnp.exp2((s - m_new).astype(MXU_DT))
                acc_s[...] = jnp.exp2(m_prev - m_new) * acc_s[...] + _dot(vt, p)
                m_s[...] = m_new

        def far_tile(i, carry):
            update(pl.multiple_of(TQ + i * SEL_TK, LANES), SEL_TK)
            return carry

        n_far = jnp.maximum(t0 - TQ, 0)
        n_full = n_far // SEL_TK
        lax.fori_loop(0, n_full, far_tile, 0)
        pos = n_full * SEL_TK
        width = SEL_TK // 2
        while width >= LANES:
            digit = ((n_far - n_full * SEL_TK) & width) != 0
            pl.when(digit)(functools.partial(update, pl.multiple_of(TQ + pos, LANES), width))
            pos = pos + jnp.where(digit, width, 0)
            width //= 2
        update(t0, DIAG_W, tsel_ref)
        acc = acc_s[...]
        osel_s[...] = acc[0:LANES] * (1.0 / acc[LANES:LANES + 1])

    lax.cond(bounded, lambda: attend(True), lambda: attend(False))
    o_s = osel_s[...]

    gsig_t = (1.0 / (1.0 + jnp.exp(-gt_ref[...]))).T
    row_lo = lax.broadcasted_iota(jnp.int32, (LANES, 1), 0) < NSA_HEAD_DIM
    for j in range(NSA_HPG):
        c0, c1 = j * TQ, (NSA_HPG + j) * TQ
        acc = None
        for br, o in enumerate((o_c, o_s, o_w)):
            r0, r1 = br * NSA_HEADS + j, br * NSA_HEADS + NSA_HPG + j
            gate = jnp.where(row_lo, gsig_t[r0:r0 + 1, :], gsig_t[r1:r1 + 1, :])
            term = gate * jnp.where(row_lo, o[:, c0:c0 + TQ], o[:, c1:c1 + TQ])
            acc = term if acc is None else acc + term
        o_ref[:, j * LANES:(j + 1) * LANES] = acc.T.astype(o_ref.dtype)


def _nsa(proj_n, gates, kcmp, vcmp_t, tsel, twin, tcmp, eind, agg_t, cind, B, S):
    nq = S // TQ
    n_cmp_pad = S // CMP_STRIDE
    qcols = NSA_WIDTH // LANES
    tok = lambda b, t: (b * nq + t, 0)
    kv = lambda c: pl.BlockSpec((S, KV_W), lambda b, t: (b, qcols + c))
    const = lambda a: pl.BlockSpec(a.shape, lambda b, t: (0,) * a.ndim)
    return pl.pallas_call(
        _nsa_kernel,
        grid=(B, nq),
        in_specs=[pl.BlockSpec((TQ, NSA_WIDTH), tok),
                  pl.BlockSpec((TQ, LANES), tok),
                  pl.BlockSpec((None, n_cmp_pad, KV_W), lambda b, t: (b, 0, 0)),
                  pl.BlockSpec((None, KV_W, n_cmp_pad), lambda b, t: (b, 0, 0)),
                  kv(0), kv(1), kv(2), kv(3),
                  const(tsel), const(twin), const(tcmp), const(eind), const(agg_t), const(cind),
                  pl.BlockSpec(memory_space=pltpu.SMEM)],
        out_specs=pl.BlockSpec((TQ, NSA_WIDTH), tok),
        out_shape=jax.ShapeDtypeStruct((B * S, NSA_WIDTH), MXU_DT),
        scratch_shapes=[pltpu.VMEM((TQ + S, 2 * LANES), MXU_DT),
                        pltpu.VMEM((VROWS, TQ + S), MXU_DT),
                        pltpu.VMEM((WINDOW + S, 2 * LANES), MXU_DT),
                        pltpu.VMEM((VROWS, WINDOW + S), MXU_DT),
                        pltpu.VMEM((n_cmp_pad, 2 * LANES), MXU_DT),
                        pltpu.VMEM((n_cmp_pad, COLS), F32),
                        pltpu.VMEM((1, COLS), F32),
                        pltpu.VMEM((VROWS, COLS), F32),
                        pltpu.VMEM((LANES, COLS), F32),
                        pltpu.VMEM((WIN_W, COLS), F32),
                        pltpu.SMEM((2 * NSA_KV_GROUPS,), F32)],
        compiler_params=_cparams(("arbitrary", "arbitrary")),
        name="nsa_attention",
    )(proj_n, gates, kcmp, vcmp_t, proj_n, proj_n, proj_n, proj_n, tsel, twin, tcmp, eind, agg_t, cind,
      jnp.stack([jnp.max(tsel), jnp.max(twin)]))


POST_SPLIT = 1
POST_TF = 256


def _layer_norm(x, g, b):
    mu = jnp.mean(x, axis=-1, keepdims=True)
    xc = x - mu
    var = jnp.mean(xc * xc, axis=-1, keepdims=True)
    return xc * lax.rsqrt(var + 1e-5) * g + b


def _post_kernel(x_ref, oa_ref, ob_ref, woa_ref, wob_ref, g1_ref, b1_ref, wg_ref, wu_ref, wd_ref, g2_ref, b2_ref,
                 o_ref, acc_ref):
    tm = x_ref.shape[0]
    halves = [slice(r, r + tm // POST_SPLIT) for r in range(0, tm, tm // POST_SPLIT)]
    x1 = []
    for r in halves:
        mix = _dot(oa_ref[r, :], woa_ref[...]) + _dot(ob_ref[r, :], wob_ref[...])
        x1.append(_layer_norm(ALPHA * x_ref[r, :] + mix, g1_ref[...], b1_ref[...]))
    xb = [v.astype(MXU_DT) for v in x1]
    tf = POST_TF
    for c in range(0, D_FF, tf):
        for r, xh in zip(halves, xb):
            gate = _dot(xh, wg_ref[:, c:c + tf])
            up = _dot(xh, wu_ref[:, c:c + tf])
            h = (gate / (1.0 + jnp.exp(-gate)) * up).astype(MXU_DT)
            d = _dot(h, wd_ref[c:c + tf, :])
            if c == 0:
                acc_ref[r, :] = d
            else:
                acc_ref[r, :] += d
    for r, v in zip(halves, x1):
        o_ref[r, :] = _layer_norm(ALPHA * v + acc_ref[r, :], g2_ref[...], b2_ref[...])


def _post(x2, oa, ob, woa, wob, g1, b1, wg, wu, wd, g2, b2, layer, tm=512):
    T = x2.shape[0]
    row = lambda i: (i, 0)
    const = lambda a: pl.BlockSpec((None,) + a.shape[1:], lambda i: (layer, 0, 0), pipeline_mode=pl.Buffered(1))
    return pl.pallas_call(
        _post_kernel,
        grid=(T // tm,),
        in_specs=[pl.BlockSpec((tm, D_MODEL), row),
                  pl.BlockSpec((tm, HG_WIDTH), row),
                  pl.BlockSpec((tm, NSA_WIDTH), row),
                  const(woa), const(wob), const(g1), const(b1),
                  const(wg), const(wu), const(wd), const(g2), const(b2)],
        out_specs=pl.BlockSpec((tm, D_MODEL), row),
        out_shape=jax.ShapeDtypeStruct((T, D_MODEL), F32),
        scratch_shapes=[pltpu.VMEM((tm, D_MODEL), F32)],
        compiler_params=_cparams(("parallel",)),
        name="outproj_ffn",
    )(x2, oa, ob, woa, wob, g1, b1, wg, wu, wd, g2, b2)


def _split_w_in(w):
    depth = w.shape[0]
    o = 4 * HG_WIDTH
    wh = w[:, :, :o]
    wq = w[:, :, o:o + NSA_WIDTH].reshape(depth, D_MODEL, NSA_KV_GROUPS, NSA_HPG, NSA_HEAD_DIM)
    wq = jnp.swapaxes(wq, 2, 3).reshape(depth, D_MODEL, NSA_WIDTH)
    wq = wq * (NSA_HEAD_DIM ** -0.5 * LOG2E)
    o += NSA_WIDTH
    kc, vc, ks, vs, kw, vw = (w[:, :, o + i * KV_W:o + (i + 1) * KV_W] for i in range(6))
    o += 6 * KV_W
    wgt = jnp.pad(w[:, :, o:], ((0, 0), (0, 0), (0, LANES - 3 * NSA_HEADS)))
    wn = jnp.concatenate([wq, ks, vs, kw, vw], axis=2)
    wc = jnp.concatenate([kc, vc], axis=2)
    return wh.astype(MXU_DT), wn.astype(MXU_DT), wc.astype(MXU_DT), wgt.astype(MXU_DT)


def kernel(x, w_in, hg_lb_param, hg_norm_w, cmp_pos_k, cmp_w1_k, cmp_w2_k, cmp_pos_v, cmp_w1_v, cmp_w2_v,
           rel_bias, w_out, ln1_g, ln1_b, w_ffn_gate, w_ffn_up, w_ffn_down, ln2_g, ln2_b):
    B, S, _ = x.shape
    T = B * S
    depth = w_in.shape[0]

    p = jax.nn.softmax(hg_lb_param.astype(F32), axis=0)
    c = jnp.cumsum(p, axis=0)
    lbs = c - c[0:1]
    loglb, log1mlb, onemlb = jnp.log(lbs), jnp.log1p(-lbs), 1.0 - lbs

    amat_np, masks_np = _hgrn_consts()
    amat, masks = jnp.asarray(amat_np, MXU_DT), jnp.asarray(masks_np)
    eind, agg_t, cind = (jnp.asarray(a, MXU_DT) for a in _nsa_consts(S))
    tsel, twin, tcmp = _nsa_tables(rel_bias)

    wh, wn, wc, wgt = _split_w_in(w_in)
    woa = w_out[:, :HG_WIDTH].astype(MXU_DT)
    wob = w_out[:, HG_WIDTH:].reshape(depth, NSA_KV_GROUPS, NSA_HPG, NSA_HEAD_DIM, D_MODEL)
    wob = jnp.swapaxes(wob, 1, 2).reshape(depth, NSA_WIDTH, D_MODEL).astype(MXU_DT)
    wfg, wfu, wfd = w_ffn_gate.astype(MXU_DT), w_ffn_up.astype(MXU_DT), w_ffn_down.astype(MXU_DT)
    ln = [a[:, None, :] for a in (ln1_g, ln1_b, ln2_g, ln2_b)]

    x2 = x.reshape(T, D_MODEL).astype(F32)
    for l in range(depth):
        proj_h, proj_n, kc, vc, gates = _inproj(x2, wh, wn, wc, wgt, l)
        o_a = _hgrn(proj_h, loglb[l:l + 1], log1mlb[l:l + 1], onemlb[l:l + 1], hg_norm_w[l][None, :], amat, masks, B, S)
        shape3 = (B, S // CMP_STRIDE, CMP_STRIDE * KV_W)
        kcmp, vcmp_t = _compress(kc.reshape(shape3), vc.reshape(shape3),
                                 _compress_params(cmp_pos_k[l], cmp_w1_k[l], cmp_w2_k[l]),
                                 _compress_params(cmp_pos_v[l], cmp_w1_v[l], cmp_w2_v[l]))
        o_b = _nsa(proj_n, gates, kcmp, vcmp_t, tsel, twin, tcmp, eind, agg_t, cind, B, S)
        x2 = _post(x2, o_a, o_b, woa, wob, ln[0], ln[1], wfg, wfu, wfd, ln[2], ln[3], l)
    return x2.reshape(B, S, D_MODEL).astype(x.dtype)
```

```python
import functools
import math

import numpy as np
import jax
import jax.numpy as jnp
from jax import lax
from jax.experimental import pallas as pl
from jax.experimental.pallas import tpu as pltpu

F32 = jnp.float32
MXU_DT = jnp.bfloat16

D_MODEL = 1024
DEPTH = 4
HG_WIDTH = 512
HG_HEAD_DIM = 128
HG_HEADS = 4
HG_CHUNK = 64
NSA_WIDTH = 512
NSA_HEAD_DIM = 64
NSA_HEADS = 8
NSA_KV_GROUPS = 2
NSA_HPG = 4
CMP_BLOCK = 32
CMP_STRIDE = 16
CMP_HIDDEN = 128
SLC_BLOCK = 64
SLC_TOPK = 16
SLC_LOCAL = 2
WINDOW = 512
N_BUCKETS = 32
MAX_DISTANCE = 128
D_FF = 2816
ALPHA = (2.0 * DEPTH) ** 0.25
KV_W = NSA_KV_GROUPS * NSA_HEAD_DIM
LANES = 128
SUBLANES = 8
BIG = 1e9
MASK_NEG = -1e30
SEL_NEG = MASK_NEG
LOG2E = math.log2(math.e)

VMEM_LIMIT = 56 * 1024 * 1024

TQ = 128
COLS = NSA_HEADS * TQ
SEL_TK = 1024
BOUND_LIMIT = 40.0
BOUND_SLACK = 1.01
VROWS = LANES + 16
DIAG_W = 2 * TQ
WIN_W = WINDOW + TQ
CMP_BAND = 24
CMP_TAB = 40
HEAD_PERM = tuple(g * NSA_HPG + j for j in range(NSA_HPG) for g in range(NSA_KV_GROUPS))


def _dot(a, b):
    return jnp.dot(a, b, preferred_element_type=F32)


def _dot_nt(a, b):
    return lax.dot_general(a, b, (((1,), (1,)), ((), ())), preferred_element_type=F32)


def _dot_tn(a, b):
    return lax.dot_general(a, b, (((0,), (0,)), ((), ())), preferred_element_type=F32)


def _dot_split(x, w, parts, x_is_rhs=False):
    acc = None
    r = x
    for p in range(parts):
        h = r.astype(MXU_DT)
        d = _dot(w, h) if x_is_rhs else _dot(h, w)
        acc = d if acc is None else acc + d
        if p + 1 < parts:
            r = r - h.astype(F32)
    return acc


def _cparams(sem):
    return pltpu.CompilerParams(dimension_semantics=sem, vmem_limit_bytes=VMEM_LIMIT)


def _inproj_kernel(x_ref, wh_ref, wn_ref, wc_ref, wg_ref, oh_ref, on_ref, okc_ref, ovc_ref, og_ref):
    xb = x_ref[...].astype(MXU_DT)
    nh = oh_ref.shape[1]
    for c in range(0, nh, 512):
        oh_ref[:, c:c + 512] = _dot(xb, wh_ref[:, c:c + 512])
    nn = on_ref.shape[1]
    for c in range(0, nn, 512):
        w = min(512, nn - c)
        on_ref[:, c:c + w] = _dot(xb, wn_ref[:, c:c + w]).astype(on_ref.dtype)
    kv = _dot(xb, wc_ref[...])
    okc_ref[...] = kv[:, :KV_W].astype(okc_ref.dtype)
    ovc_ref[...] = kv[:, KV_W:].astype(ovc_ref.dtype)
    og_ref[...] = _dot(xb, wg_ref[...])


def _inproj(x2, wh, wn, wc, wg, layer, tm=512):
    T = x2.shape[0]
    nh, nn = wh.shape[2], wn.shape[2]
    full = lambda i: (layer, 0, 0)
    row = lambda i: (i, 0)
    return pl.pallas_call(
        _inproj_kernel,
        grid=(T // tm,),
        in_specs=[pl.BlockSpec((tm, D_MODEL), row),
                  pl.BlockSpec((None, D_MODEL, nh), full),
                  pl.BlockSpec((None, D_MODEL, nn), full),
                  pl.BlockSpec((None, D_MODEL, 2 * KV_W), full),
                  pl.BlockSpec((None, D_MODEL, LANES), full)],
        out_specs=[pl.BlockSpec((tm, nh), row),
                   pl.BlockSpec((tm, nn), row),
                   pl.BlockSpec((tm, KV_W), row),
                   pl.BlockSpec((tm, KV_W), row),
                   pl.BlockSpec((tm, LANES), row)],
        out_shape=[jax.ShapeDtypeStruct((T, nh), F32),
                   jax.ShapeDtypeStruct((T, nn), MXU_DT),
                   jax.ShapeDtypeStruct((T, KV_W), MXU_DT),
                   jax.ShapeDtypeStruct((T, KV_W), MXU_DT),
                   jax.ShapeDtypeStruct((T, LANES), F32)],
        compiler_params=_cparams(("parallel",)),
        name="inproj",
    )(x2, wh, wn, wc, wg)


_HG_LEVELS = (32, 16, 8, 4, 2, 1)
_HG_NLEV = len(_HG_LEVELS)
HG_FAST_LIMIT = -80.0
HG_PREP_ROWS = 16


def _hgrn_consts():
    C = HG_CHUNK
    a = np.zeros((8 * C, C), np.float32)
    m = np.zeros((_HG_NLEV + 1, C, C), np.float32)
    for l, h in enumerate(_HG_LEVELS):
        for r in range(C):
            mid = (r // (2 * h)) * 2 * h + h
            if r >= mid:
                a[l * C + r, mid:r + 1] = 1.0
                m[l, r, mid - h:mid] = 1.0
            else:
                a[l * C + r, r + 1:mid] = -1.0
    for r in range(C):
        a[6 * C + r, :r + 1] = 1.0
        a[7 * C + r, r + 1:] = 1.0
        m[_HG_NLEV, r, r] = 1.0
    return a, m


def _hgrn_kernel(q_ref, f_ref, i_ref, g_ref, loglb_ref, log1mlb_ref, onemlb_ref, gw_ref, amat_ref, mask_ref,
                 o_ref, state_ref, logf_s, k_s):
    C, Dk = HG_CHUNK, HG_HEAD_DIM
    ts = q_ref.shape[0]

    @pl.when(pl.program_id(1) == 0)
    def _():
        state_ref[...] = jnp.zeros_like(state_ref)

    a = loglb_ref[...]
    for r in range(0, ts, HG_PREP_ROWS):
        rows = slice(r, r + HG_PREP_ROWS)
        z = f_ref[rows, :]
        e = jnp.exp(-jnp.abs(z))
        e1 = 1.0 + e
        bt = log1mlb_ref[...] + (jnp.minimum(z, 0.0) - jnp.log(e1))
        logf_s[rows, :] = jnp.maximum(a, bt) + jnp.log(1.0 + jnp.exp(-jnp.abs(a - bt)))
        k_s[rows, :] = onemlb_ref[...] * jnp.where(z > 0.0, e, 1.0) / e1

    def finish(o, rows, cols):
        o = o * lax.rsqrt(jnp.mean(o * o, axis=-1, keepdims=True) + 1e-6) * gw_ref[...]
        g = g_ref[rows, cols]
        o_ref[rows, cols] = (o * (g / (1.0 + jnp.exp(-g)))).astype(o_ref.dtype)

    def robust():
        amat = amat_ref[...]

        def chunk(c, carry):
            r0 = pl.multiple_of(c * C, C)
            rows = pl.ds(r0, C)
            for h in range(HG_HEADS):
                cols = slice(h * Dk, (h + 1) * Dk)
                d = _dot_split(logf_s[rows, cols], amat, 2, x_is_rhs=True)
                q = q_ref[rows, cols]
                k = k_s[rows, cols]
                v = i_ref[rows, cols].astype(MXU_DT)
                sc = _dot_nt(q.astype(MXU_DT), k.astype(MXU_DT)) * mask_ref[_HG_NLEV]
                for l in range(_HG_NLEV):
                    el = jnp.exp(-jnp.abs(d[l * C:(l + 1) * C]))
                    sc = sc + _dot_nt((q * el).astype(MXU_DT), (k * el).astype(MXU_DT)) * mask_ref[l]
                b = d[6 * C:7 * C]
                st = state_ref[h]
                o = _dot(sc.astype(MXU_DT), v) + _dot_nt((q * jnp.exp(b)).astype(MXU_DT), st.astype(MXU_DT))
                kdec = (k * jnp.exp(d[7 * C:8 * C])).astype(MXU_DT)
                state_ref[h] = st * jnp.exp(b[C - 1:C]) + _dot_tn(v, kdec)
                finish(o, rows, cols)
            return carry

        lax.fori_loop(0, ts // C, chunk, 0)

    def fast():
        tril_mat = amat_ref[6 * C:7 * C, :]
        causal = lax.broadcasted_iota(jnp.int32, (C, C), 0) >= lax.broadcasted_iota(jnp.int32, (C, C), 1)
        n = ts // C
        chunks = [slice(c * C, (c + 1) * C) for c in range(n)]
        heads = [slice(h * Dk, (h + 1) * Dk) for h in range(HG_HEADS)]
        b = [_dot_split(logf_s[r, :], tril_mat, 2, x_is_rhs=True) for r in chunks]
        e = [jnp.exp(x) for x in b]
        dec = [x[C - 1:C] for x in e]
        qe = [(q_ref[r, :] * x).astype(MXU_DT) for r, x in zip(chunks, e)]
        kt = [k_s[r, :] * jnp.exp(-x) for r, x in zip(chunks, b)]
        ktb = [x.astype(MXU_DT) for x in kt]
        kdec = [(x * d).astype(MXU_DT) for x, d in zip(kt, dec)]
        v = [i_ref[r, :].astype(MXU_DT) for r in chunks]
        o_intra = [[_dot(jnp.where(causal, _dot_nt(qe[c][:, h], ktb[c][:, h]), 0.0).astype(MXU_DT), v[c][:, h])
                    for h in heads] for c in range(n)]
        upd = [[_dot_tn(v[c][:, h], kdec[c][:, h]) for h in heads] for c in range(n)]
        states = [[state_ref[i] for i in range(HG_HEADS)]]
        for c in range(n):
            states.append([states[c][i] * dec[c][:, h] + upd[c][i] for i, h in enumerate(heads)])
        for i in range(HG_HEADS):
            state_ref[i] = states[n][i]
        for c in range(n):
            for i, h in enumerate(heads):
                finish(o_intra[c][i] + _dot_nt(qe[c][:, h], states[c][i].astype(MXU_DT)), chunks[c], h)

    lf = logf_s[...]
    total = jnp.min(jnp.concatenate(
        [jnp.sum(lf[c * C:(c + 1) * C], axis=0, keepdims=True) for c in range(ts // C)], axis=0))
    lax.cond(total >= HG_FAST_LIMIT, fast, robust)


def _hgrn(proj_h, loglb, log1mlb, onemlb, gw, amat, masks, B, S, ts=512):
    nt = S // ts
    sec = lambda s: (lambda b, t: (b * nt + t, s))
    vec = pl.BlockSpec((1, HG_WIDTH), lambda b, t: (0, 0))
    return pl.pallas_call(
        _hgrn_kernel,
        grid=(B, nt),
        in_specs=[pl.BlockSpec((ts, HG_WIDTH), sec(0)),
                  pl.BlockSpec((ts, HG_WIDTH), sec(1)),
                  pl.BlockSpec((ts, HG_WIDTH), sec(2)),
                  pl.BlockSpec((ts, HG_WIDTH), sec(3)),
                  vec, vec, vec,
                  pl.BlockSpec((1, HG_HEAD_DIM), lambda b, t: (0, 0)),
                  pl.BlockSpec(amat.shape, lambda b, t: (0, 0)),
                  pl.BlockSpec(masks.shape, lambda b, t: (0, 0, 0))],
        out_specs=pl.BlockSpec((ts, HG_WIDTH), lambda b, t: (b * nt + t, 0)),
        out_shape=jax.ShapeDtypeStruct((B * S, HG_WIDTH), MXU_DT),
        scratch_shapes=[pltpu.VMEM((HG_HEADS, HG_HEAD_DIM, HG_HEAD_DIM), F32),
                        pltpu.VMEM((ts, HG_WIDTH), F32),
                        pltpu.VMEM((ts, HG_WIDTH), F32)],
        compiler_params=_cparams(("arbitrary", "arbitrary")),
        name="hgrn2",
    )(proj_h, proj_h, proj_h, proj_h, loglb, log1mlb, onemlb, gw, amat, masks)


def _gelu_tanh(x):
    return 0.5 * x * (1.0 + jnp.tanh(math.sqrt(2.0 / math.pi) * (x + 0.044715 * (x * x * x))))


def _compress_kernel(kc_ref, vc_ref, pka_ref, pkb_ref, wka_ref, wkb_ref, wk2_ref,
                     pva_ref, pvb_ref, wva_ref, wvb_ref, wv2_ref, ok_ref, ov_ref):
    def one(x_ref, pa_ref, pb_ref, wa_ref, wb_ref, w2_ref):
        x = x_ref[...]
        n = x.shape[0]
        ha = _dot(x, wa_ref[...])
        hb = _dot(x, wb_ref[...])
        posw = _dot(pa_ref[...], wa_ref[...]) + _dot(pb_ref[...], wb_ref[...])
        h = ha + pltpu.roll(hb, n - 1, axis=0) + posw[0:1]
        return _dot(_gelu_tanh(h).astype(MXU_DT), w2_ref[...])

    ok_ref[...] = one(kc_ref, pka_ref, pkb_ref, wka_ref, wkb_ref, wk2_ref).astype(ok_ref.dtype)
    ov_ref[...] = one(vc_ref, pva_ref, pvb_ref, wva_ref, wvb_ref, wv2_ref).T.astype(ov_ref.dtype)


def _compress(kc3, vc3, kparams, vparams):
    B, n, w = kc3.shape
    x_spec = pl.BlockSpec((None, n, w), lambda b: (b, 0, 0))
    full = lambda a: pl.BlockSpec(a.shape, lambda b: (0,) * a.ndim)
    return pl.pallas_call(
        _compress_kernel,
        grid=(B,),
        in_specs=[x_spec, x_spec] + [full(a) for a in kparams] + [full(a) for a in vparams],
        out_specs=[pl.BlockSpec((None, n, KV_W), lambda b: (b, 0, 0)),
                   pl.BlockSpec((None, KV_W, n), lambda b: (b, 0, 0))],
        out_shape=[jax.ShapeDtypeStruct((B, n, KV_W), MXU_DT),
                   jax.ShapeDtypeStruct((B, KV_W, n), MXU_DT)],
        compiler_params=_cparams(("parallel",)),
        name="nsa_compress",
    )(kc3, vc3, *kparams, *vparams)


def _compress_params(pos, w1, w2):
    G, Dh, Hd = NSA_KV_GROUPS, NSA_HEAD_DIM, CMP_HIDDEN
    eye = jnp.eye(G, dtype=F32)

    def half(wh, ph):
        wbig = jnp.einsum('rdj,gk->rgdkj', wh.reshape(CMP_STRIDE, Dh, Hd), eye).reshape(CMP_STRIDE * G * Dh, G * Hd)
        pbig = jnp.broadcast_to(ph[:, None, :], (CMP_STRIDE, G, Dh)).reshape(1, CMP_STRIDE * G * Dh)
        return jnp.broadcast_to(pbig, (8, pbig.shape[1])).astype(MXU_DT), wbig.astype(MXU_DT)

    pa, wa = half(w1[:CMP_STRIDE * Dh], pos[:CMP_STRIDE])
    pb, wb = half(w1[CMP_STRIDE * Dh:], pos[CMP_STRIDE:])
    w2big = jnp.einsum('jd,gk->gjkd', w2, eye).reshape(G * Hd, G * Dh).astype(MXU_DT)
    return pa, pb, wa, wb, w2big


def _t5_bucket_np(n):
    n = np.maximum(n, 0)
    max_exact = N_BUCKETS // 2
    nf = np.maximum(n, 1).astype(np.float32)
    large = max_exact + (np.log(nf / np.float32(max_exact)) / np.float32(math.log(MAX_DISTANCE / max_exact))
                         * np.float32(N_BUCKETS - max_exact)).astype(np.int32)
    large = np.minimum(large, N_BUCKETS - 1)
    return np.where(n < max_exact, n, large).astype(np.int32)


def _nsa_tables(rel_bias):
    rb = rel_bias.astype(F32)

    def per_distance(dist, valid, rel_to_far):
        b = rb[jnp.asarray(_t5_bucket_np(dist))]
        if rel_to_far:
            b = b - rb[N_BUCKETS - 1]
        return (jnp.where(jnp.asarray(valid)[:, None], b, MASK_NEG) * LOG2E).T

    def skewed(dist_of, n_rows, n_cols, step, valid, rel_to_far):
        neg = step * (n_rows - 1)
        period = n_cols + neg + step
        k = np.arange(period)
        k = np.where(k < n_cols + step, k, k - period)
        d = dist_of(k)
        f = per_distance(d, valid(d), rel_to_far)
        flat = jnp.tile(f, (1, n_rows))[:, :n_rows * (period - step)]
        return flat.reshape(NSA_HEADS, n_rows, period - step)[:, :, :n_cols]

    def key_major(n_keys, first_dist, valid, rel_to_far):
        m = skewed(lambda k: first_dist - k, TQ, n_keys, 1, valid, rel_to_far)
        return jnp.transpose(m, (2, 0, 1)).reshape(n_keys, COLS)

    t_sel = key_major(DIAG_W, TQ, lambda d: d >= 0, True)
    t_win = key_major(WIN_W, WINDOW, lambda d: (d >= 0) & (d < WINDOW), False)
    m = skewed(lambda k: k + CMP_STRIDE * 16 - (CMP_BLOCK - 1), CMP_TAB, TQ, CMP_STRIDE, lambda d: d >= 0, True)
    t_cmp = jnp.transpose(m, (1, 0, 2)).reshape(CMP_TAB, COLS)
    return t_sel, t_win, t_cmp


def _nsa_consts(S):
    n_slc = S // SLC_BLOCK
    n_cmp_pad = S // CMP_STRIDE
    assert n_slc < LANES and n_cmp_pad >= CMP_BAND
    far = _t5_bucket_np(np.arange(8 * MAX_DISTANCE)) == N_BUCKETS - 1
    far_dist = int(np.max(np.nonzero(~far)[0])) + 1
    assert TQ + 1 >= far_dist and 17 * CMP_STRIDE - (CMP_BLOCK - 1) >= far_dist
    eind = np.zeros((S, LANES), np.float32)
    eind[np.arange(S), np.arange(S) // SLC_BLOCK] = 1.0
    cs = np.arange(n_cmp_pad)[None, :] * CMP_STRIDE
    ss = np.arange(LANES)[:, None] * SLC_BLOCK
    ov = np.clip(np.minimum(cs + CMP_BLOCK, ss + SLC_BLOCK) - np.maximum(cs, ss), 0, None)
    agg_t = (ov / CMP_STRIDE).astype(np.float32)
    agg_t[:, n_cmp_pad - 1] = 0.0
    agg_t[n_slc:] = 0.0
    assert n_cmp_pad <= SUBLANES * LANES
    cind = np.zeros((n_cmp_pad, LANES), np.float32)
    cind[np.arange(n_cmp_pad), np.arange(n_cmp_pad) // SUBLANES] = 1.0
    return eind, agg_t, cind


def _nsa_kernel(q_ref, gt_ref, kc_ref, vct_ref, ks_ref, vs_ref, kw_ref, vw_ref,
                tsel_ref, twin_ref, tcmp_ref, eind_ref, aggt_ref, cind_ref, tmax_ref,
                o_ref, kaug_s, vselt_s, kwin_s, vwint_s, kcaug_s, sc_s, m_s, acc_s, osel_s, sw_s, knorm_s):
    S = ks_ref.shape[0]
    n_slc = S // SLC_BLOCK
    n_cmp_pad = kc_ref.shape[0]
    qt = pl.program_id(1)
    t0 = pl.multiple_of(qt * TQ, TQ)

    @pl.when(qt == 0)
    def _():
        lane1 = lax.broadcasted_iota(jnp.int32, (1, LANES), 1)
        kaug_s[0:TQ, 0:LANES] = jnp.zeros((TQ, LANES), MXU_DT)
        kaug_s[0:TQ, LANES:] = jnp.broadcast_to(jnp.where(lane1 == LANES - 1, 1.0, 0.0).astype(MXU_DT), (TQ, LANES))
        kaug_s[TQ:, 0:LANES] = ks_ref[...]
        kaug_s[TQ:, LANES:] = eind_ref[...]
        kcaug_s[:, 0:LANES] = kc_ref[...]
        kcaug_s[:, LANES:] = cind_ref[...]
        rg = lax.broadcasted_iota(jnp.int32, (LANES, LANES), 0) // NSA_HEAD_DIM
        cg = lax.broadcasted_iota(jnp.int32, (LANES, LANES), 1) // NSA_HEAD_DIM
        same_group = jnp.where(rg == cg, 1.0, 0.0).astype(MXU_DT)
        for i, ref in enumerate((ks_ref, kw_ref)):
            n2 = _dot(jnp.square(ref[...].astype(F32)).astype(MXU_DT), same_group)
            n2 = jnp.max(n2, axis=0, keepdims=True)
            for g in range(NSA_KV_GROUPS):
                knorm_s[i * NSA_KV_GROUPS + g] = jnp.sqrt(jnp.max(n2[:, g * NSA_HEAD_DIM:(g + 1) * NSA_HEAD_DIM]))
        ones_rows = (lax.broadcasted_iota(jnp.int32, (VROWS - LANES, 1), 0) == 0).astype(MXU_DT)
        vselt_s[0:LANES, 0:TQ] = jnp.zeros((LANES, TQ), MXU_DT)
        vselt_s[0:LANES, TQ:] = vs_ref[...].astype(F32).T.astype(MXU_DT)
        vselt_s[LANES:, :] = jnp.broadcast_to(ones_rows, (VROWS - LANES, TQ + S))
        kwin_s[0:WINDOW, 0:LANES] = jnp.zeros((WINDOW, LANES), MXU_DT)
        kwin_s[0:WINDOW, LANES:] = jnp.broadcast_to(jnp.where(lane1 == 0, 1.0, 0.0).astype(MXU_DT), (WINDOW, LANES))
        kwin_s[WINDOW:, 0:LANES] = kw_ref[...]
        kwin_s[WINDOW:, LANES:] = jnp.zeros((S, LANES), MXU_DT)
        vwint_s[0:LANES, 0:WINDOW] = jnp.zeros((LANES, WINDOW), MXU_DT)
        vwint_s[0:LANES, WINDOW:] = vw_ref[...].astype(F32).T.astype(MXU_DT)
        vwint_s[LANES:, :] = jnp.broadcast_to(ones_rows, (VROWS - LANES, WINDOW + S))

    lane = lax.broadcasted_iota(jnp.int32, (1, LANES), 1)
    lo_lane = lane < NSA_HEAD_DIM
    zero = jnp.zeros((), MXU_DT)

    qg = jnp.concatenate([jnp.where(lo_lane if g == 0 else jnp.logical_not(lo_lane),
                                    q_ref[:, j * LANES:(j + 1) * LANES], zero)
                          for g in range(NSA_KV_GROUPS) for j in range(NSA_HPG)], axis=0)

    n0 = pl.multiple_of(qt * (TQ // CMP_STRIDE), SUBLANES)
    late = jnp.where(lane > n0 // SUBLANES, MASK_NEG, 0.0).astype(MXU_DT)
    sc_s[...] = _dot_nt(kcaug_s[...], jnp.concatenate([qg, jnp.broadcast_to(late, (COLS, LANES))], axis=1))

    qn = jnp.sqrt(_dot_nt(jnp.ones((SUBLANES, LANES), MXU_DT), jnp.square(qg.astype(F32)).astype(MXU_DT))[0:1])
    first_group = lax.broadcasted_iota(jnp.int32, (1, COLS), 1) < COLS // NSA_KV_GROUPS
    b_sel = qn * jnp.where(first_group, knorm_s[0], knorm_s[1]) * BOUND_SLACK + (tmax_ref[0] + 1.0)
    b_win = qn * jnp.where(first_group, knorm_s[2], knorm_s[3]) * BOUND_SLACK + (tmax_ref[1] + 1.0)
    bounded = jnp.maximum(jnp.max(b_sel), jnp.max(b_win)) <= BOUND_LIMIT

    qw = jnp.concatenate(
        [qg, jnp.broadcast_to(jnp.where(lane == 0, SEL_NEG, 0.0).astype(MXU_DT), (COLS, LANES))], axis=1)
    sw_s[...] = _dot_nt(kwin_s[pl.ds(t0, WIN_W), :], qw) + twin_ref[...]

    band = pl.multiple_of(jnp.maximum(n0 - 16, 0), SUBLANES)
    toff = pl.multiple_of(16 + band - n0, SUBLANES)
    sc_s[pl.ds(band, CMP_BAND), :] += tcmp_ref[pl.ds(toff, CMP_BAND), :]
    s_c = sc_s[...]
    m_c = jnp.max(s_c, axis=0, keepdims=True)
    p_c = jnp.exp2(s_c - m_c)
    l_c = jnp.sum(p_c, axis=0, keepdims=True)
    p_c = p_c * jnp.where(m_c > 0.5 * MASK_NEG, 1.0 / l_c, 0.0)
    o_c = _dot(vct_ref[...], p_c.astype(MXU_DT))

    p_sum = jnp.concatenate(
        [sum(p_c[:, (g * NSA_HPG + j) * TQ:(g * NSA_HPG + j + 1) * TQ] for j in range(NSA_HPG))
         for g in range(NSA_KV_GROUPS)], axis=1)
    n_rows = -(-n_slc // SUBLANES) * SUBLANES
    imp = _dot_split(p_sum, aggt_ref[...], 3, x_is_rhs=True)[0:n_rows]
    blk = lax.broadcasted_iota(jnp.int32, (n_rows, 1), 0)
    blk_f = blk.astype(F32)
    tq_pos = t0 + lax.broadcasted_iota(jnp.int32, (1, TQ), 1)
    tpos = jnp.concatenate([tq_pos] * NSA_KV_GROUPS, axis=1)
    diff = tpos // SLC_BLOCK - blk
    forced = (blk == 0) | ((diff >= 0) & (diff < SLC_LOCAL))
    score = jnp.where(forced, -jnp.inf, jnp.where(blk * SLC_BLOCK <= tpos, imp, -BIG))
    score = jnp.where(blk < n_slc, score, -jnp.inf)
    ref_win = lax.cond(bounded, lambda: b_win, lambda: jnp.max(sw_s[...], axis=0, keepdims=True))
    win_pieces = list(range(0, WIN_W, LANES))
    acc_w = None
    n_rounds = max(min(SLC_TOPK, n_slc) - (1 + SLC_LOCAL), 0)
    for rnd in range(max(n_rounds, len(win_pieces))):
        if rnd < n_rounds:
            mx = jnp.max(score, axis=0, keepdims=True)
            idx = jnp.min(jnp.where(score == mx, blk_f, float(LANES)), axis=0, keepdims=True)
            score = jnp.where(blk_f == idx, -jnp.inf, score)
        if rnd < len(win_pieces):
            r = win_pieces[rnd]
            p_w = jnp.exp2((sw_s[r:r + LANES, :] - ref_win).astype(MXU_DT))
            part = _dot(vwint_s[:, pl.ds(pl.multiple_of(t0 + r, LANES), LANES)], p_w)
            acc_w = part if acc_w is None else acc_w + part
    o_w = acc_w[0:LANES] * (1.0 / acc_w[LANES:LANES + 1])
    sel = jnp.where((score == -jnp.inf) & (blk < n_slc), 1.0, 0.0)
    if n_rows < LANES:
        sel = jnp.concatenate([sel, jnp.zeros((LANES - n_rows, sel.shape[1]), F32)], axis=0)
    selb = (sel - 1.0) * (-SEL_NEG)
    selb_t = [selb[:, g * TQ:(g + 1) * TQ].T for g in range(NSA_KV_GROUPS)]
    selb_rows = jnp.concatenate([selb_t[g] for g in range(NSA_KV_GROUPS) for _ in range(NSA_HPG)], axis=0)

    qa = jnp.concatenate([qg, selb_rows.astype(MXU_DT)], axis=1)

    def attend(use_bound):
        if not use_bound:
            m_s[...] = jnp.full(m_s.shape, MASK_NEG, F32)
        acc_s[...] = jnp.zeros(acc_s.shape, F32)

        def update(start, width, table_ref=None):
            s = _dot_nt(kaug_s[pl.ds(start, width), :], qa)
            if table_ref is not None:
                s = s + table_ref[...]
            vt = vselt_s[:, pl.ds(start, width)]
            if use_bound:
                acc_s[...] += _dot(vt, jnp.exp2((s - b_sel).astype(MXU_DT)))
            else:
                m_prev = m_s[...]
                m_new = jnp.maximum(m_prev, jnp.max(s, axis=0, keepdims=True))
                p = jnp.exp2((s - m_new).astype(MXU_DT))
                acc_s[...] = jnp.exp2(m_prev - m_new) * acc_s[...] + _dot(vt, p)
                m_s[...] = m_new

        def far_tile(i, carry):
            update(pl.multiple_of(TQ + i * SEL_TK, LANES), SEL_TK)
            return carry

        n_far = jnp.maximum(t0 - TQ, 0)
        n_full = n_far // SEL_TK
        lax.fori_loop(0, n_full, far_tile, 0)
        pos = n_full * SEL_TK
        width = SEL_TK // 2
        while width >= LANES:
            digit = ((n_far - n_full * SEL_TK) & width) != 0
            pl.when(digit)(functools.partial(update, pl.multiple_of(TQ + pos, LANES), width))
            pos = pos + jnp.where(digit, width, 0)
            width //= 2
        update(t0, DIAG_W, tsel_ref)
        acc = acc_s[...]
        osel_s[...] = acc[0:LANES] * (1.0 / acc[LANES:LANES + 1])

    lax.cond(bounded, lambda: attend(True), lambda: attend(False))
    o_s = osel_s[...]

    gsig_t = (1.0 / (1.0 + jnp.exp(-gt_ref[...]))).T
    row_lo = lax.broadcasted_iota(jnp.int32, (LANES, 1), 0) < NSA_HEAD_DIM
    for j in range(NSA_HPG):
        c0, c1 = j * TQ, (NSA_HPG + j) * TQ
        acc = None
        for br, o in enumerate((o_c, o_s, o_w)):
            r0, r1 = br * NSA_HEADS + j, br * NSA_HEADS + NSA_HPG + j
            gate = jnp.where(row_lo, gsig_t[r0:r0 + 1, :], gsig_t[r1:r1 + 1, :])
            term = gate * jnp.where(row_lo, o[:, c0:c0 + TQ], o[:, c1:c1 + TQ])
            acc = term if acc is None else acc + term
        o_ref[:, j * LANES:(j + 1) * LANES] = acc.T.astype(o_ref.dtype)


def _nsa(proj_n, gates, kcmp, vcmp_t, tsel, twin, tcmp, eind, agg_t, cind, B, S):
    nq = S // TQ
    n_cmp_pad = S // CMP_STRIDE
    qcols = NSA_WIDTH // LANES
    tok = lambda b, t: (b * nq + t, 0)
    kv = lambda c: pl.BlockSpec((S, KV_W), lambda b, t: (b, qcols + c))
    const = lambda a: pl.BlockSpec(a.shape, lambda b, t: (0,) * a.ndim)
    return pl.pallas_call(
        _nsa_kernel,
        grid=(B, nq),
        in_specs=[pl.BlockSpec((TQ, NSA_WIDTH), tok),
                  pl.BlockSpec((TQ, LANES), tok),
                  pl.BlockSpec((None, n_cmp_pad, KV_W), lambda b, t: (b, 0, 0)),
                  pl.BlockSpec((None, KV_W, n_cmp_pad), lambda b, t: (b, 0, 0)),
                  kv(0), kv(1), kv(2), kv(3),
                  const(tsel), const(twin), const(tcmp), const(eind), const(agg_t), const(cind),
                  pl.BlockSpec(memory_space=pltpu.SMEM)],
        out_specs=pl.BlockSpec((TQ, NSA_WIDTH), tok),
        out_shape=jax.ShapeDtypeStruct((B * S, NSA_WIDTH), MXU_DT),
        scratch_shapes=[pltpu.VMEM((TQ + S, 2 * LANES), MXU_DT),
                        pltpu.VMEM((VROWS, TQ + S), MXU_DT),
                        pltpu.VMEM((WINDOW + S, 2 * LANES), MXU_DT),
                        pltpu.VMEM((VROWS, WINDOW + S), MXU_DT),
                        pltpu.VMEM((n_cmp_pad, 2 * LANES), MXU_DT),
                        pltpu.VMEM((n_cmp_pad, COLS), F32),
                        pltpu.VMEM((1, COLS), F32),
                        pltpu.VMEM((VROWS, COLS), F32),
                        pltpu.VMEM((LANES, COLS), F32),
                        pltpu.VMEM((WIN_W, COLS), F32),
                        pltpu.SMEM((2 * NSA_KV_GROUPS,), F32)],
        compiler_params=_cparams(("arbitrary", "arbitrary")),
        name="nsa_attention",
    )(proj_n, gates, kcmp, vcmp_t, proj_n, proj_n, proj_n, proj_n, tsel, twin, tcmp, eind, agg_t, cind,
      jnp.stack([jnp.max(tsel), jnp.max(twin)]))


POST_TF = 256


def _layer_norm(x, g, b):
    mu = jnp.mean(x, axis=-1, keepdims=True)
    xc = x - mu
    var = jnp.mean(xc * xc, axis=-1, keepdims=True)
    return xc * lax.rsqrt(var + 1e-5) * g + b


def _post_kernel(x_ref, oa_ref, ob_ref, woa_ref, wob_ref, g1_ref, b1_ref, wg_ref, wu_ref, wd_ref, g2_ref, b2_ref,
                 o_ref, acc_ref):
    mix = _dot(oa_ref[...], woa_ref[...]) + _dot(ob_ref[...], wob_ref[...])
    x1 = _layer_norm(ALPHA * x_ref[...] + mix, g1_ref[...], b1_ref[...])
    xb = x1.astype(MXU_DT)
    for c in range(0, D_FF, POST_TF):
        gate = _dot(xb, wg_ref[:, c:c + POST_TF])
        up = _dot(xb, wu_ref[:, c:c + POST_TF])
        h = (gate / (1.0 + jnp.exp(-gate)) * up).astype(MXU_DT)
        d = _dot(h, wd_ref[c:c + POST_TF, :])
        if c == 0:
            acc_ref[...] = d
        else:
            acc_ref[...] += d
    o_ref[...] = _layer_norm(ALPHA * x1 + acc_ref[...], g2_ref[...], b2_ref[...])


def _post(x2, oa, ob, woa, wob, g1, b1, wg, wu, wd, g2, b2, layer, tm=512):
    T = x2.shape[0]
    row = lambda i: (i, 0)
    const = lambda a: pl.BlockSpec((None,) + a.shape[1:], lambda i: (layer, 0, 0), pipeline_mode=pl.Buffered(1))
    return pl.pallas_call(
        _post_kernel,
        grid=(T // tm,),
        in_specs=[pl.BlockSpec((tm, D_MODEL), row),
                  pl.BlockSpec((tm, HG_WIDTH), row),
                  pl.BlockSpec((tm, NSA_WIDTH), row),
                  const(woa), const(wob), const(g1), const(b1),
                  const(wg), const(wu), const(wd), const(g2), const(b2)],
        out_specs=pl.BlockSpec((tm, D_MODEL), row),
        out_shape=jax.ShapeDtypeStruct((T, D_MODEL), F32),
        scratch_shapes=[pltpu.VMEM((tm, D_MODEL), F32)],
        compiler_params=_cparams(("parallel",)),
        name="outproj_ffn",
    )(x2, oa, ob, woa, wob, g1, b1, wg, wu, wd, g2, b2)


def _split_w_in(w):
    depth = w.shape[0]
    o = 4 * HG_WIDTH
    wh = w[:, :, :o]
    wq = w[:, :, o:o + NSA_WIDTH].reshape(depth, D_MODEL, NSA_KV_GROUPS, NSA_HPG, NSA_HEAD_DIM)
    wq = jnp.swapaxes(wq, 2, 3).reshape(depth, D_MODEL, NSA_WIDTH)
    wq = wq * (NSA_HEAD_DIM ** -0.5 * LOG2E)
    o += NSA_WIDTH
    kc, vc, ks, vs, kw, vw = (w[:, :, o + i * KV_W:o + (i + 1) * KV_W] for i in range(6))
    o += 6 * KV_W
    wgt = jnp.pad(w[:, :, o:], ((0, 0), (0, 0), (0, LANES - 3 * NSA_HEADS)))
    wn = jnp.concatenate([wq, ks, vs, kw, vw], axis=2)
    wc = jnp.concatenate([kc, vc], axis=2)
    return wh.astype(MXU_DT), wn.astype(MXU_DT), wc.astype(MXU_DT), wgt.astype(MXU_DT)


def kernel(x, w_in, hg_lb_param, hg_norm_w, cmp_pos_k, cmp_w1_k, cmp_w2_k, cmp_pos_v, cmp_w1_v, cmp_w2_v,
           rel_bias, w_out, ln1_g, ln1_b, w_ffn_gate, w_ffn_up, w_ffn_down, ln2_g, ln2_b):
    B, S, _ = x.shape
    T = B * S
    depth = w_in.shape[0]

    p = jax.nn.softmax(hg_lb_param.astype(F32), axis=0)
    c = jnp.cumsum(p, axis=0)
    lbs = c - c[0:1]
    loglb, log1mlb, onemlb = jnp.log(lbs), jnp.log1p(-lbs), 1.0 - lbs

    amat_np, masks_np = _hgrn_consts()
    amat, masks = jnp.asarray(amat_np, MXU_DT), jnp.asarray(masks_np)
    eind, agg_t, cind = (jnp.asarray(a, MXU_DT) for a in _nsa_consts(S))
    tsel, twin, tcmp = _nsa_tables(rel_bias)

    wh, wn, wc, wgt = _split_w_in(w_in)
    woa = w_out[:, :HG_WIDTH].astype(MXU_DT)
    wob = w_out[:, HG_WIDTH:].reshape(depth, NSA_KV_GROUPS, NSA_HPG, NSA_HEAD_DIM, D_MODEL)
    wob = jnp.swapaxes(wob, 1, 2).reshape(depth, NSA_WIDTH, D_MODEL).astype(MXU_DT)
    wfg, wfu, wfd = w_ffn_gate.astype(MXU_DT), w_ffn_up.astype(MXU_DT), w_ffn_down.astype(MXU_DT)
    ln = [a[:, None, :] for a in (ln1_g, ln1_b, ln2_g, ln2_b)]

    x2 = x.reshape(T, D_MODEL).astype(F32)
    for l in range(depth):
        proj_h, proj_n, kc, vc, gates = _inproj(x2, wh, wn, wc, wgt, l)
        o_a = _hgrn(proj_h, loglb[l:l + 1], log1mlb[l:l + 1], onemlb[l:l + 1], hg_norm_w[l][None, :], amat, masks, B, S)
        shape3 = (B, S // CMP_STRIDE, CMP_STRIDE * KV_W)
        kcmp, vcmp_t = _compress(kc.reshape(shape3), vc.reshape(shape3),
                                 _compress_params(cmp_pos_k[l], cmp_w1_k[l], cmp_w2_k[l]),
                                 _compress_params(cmp_pos_v[l], cmp_w1_v[l], cmp_w2_v[l]))
        o_b = _nsa(proj_n, gates, kcmp, vcmp_t, tsel, twin, tcmp, eind, agg_t, cind, B, S)
        x2 = _post(x2, o_a, o_b, woa, wob, ln[0], ln[1], wfg, wfu, wfd, ln[2], ln[3], l)
    return x2.reshape(B, S, D_MODEL).astype(x.dtype)
```
